```python
import math
import jax, jax.numpy as jnp
from jax import lax
import numpy as np

D_MODEL = 2048
BATCH = 4
SEQ = 4096
DEPTH = 2

HEAD_DIM = 128
GRID_W = 64
NA_HEADS = 6
NA_WIN_H = 8
NA_WIN_W = 16
DN_HEADS = 6
DN_CONV = 5
DN_CHUNK = 64
MEM_HEADS = 4
MEM_LEN = 256
DIFF_HEADS = 6
DIFF_VDIM = 2 * HEAD_DIM
Q_BLOCK = 128
ROPE_THETA = 10000.0
EPS = 1e-6
N_EVEN = (DEPTH + 1) // 2
N_ODD = DEPTH // 2

NA_W = NA_HEADS * HEAD_DIM
DN_W = DN_HEADS * HEAD_DIM
MEM_W = MEM_HEADS * HEAD_DIM
DIFF_W = DIFF_HEADS * DIFF_VDIM
MIX_W = NA_W + DN_W + MEM_W
EVEN_IN = 4 * NA_W + 4 * DN_W + 4 * DN_HEADS + 2 * MEM_W
ODD_IN = 4 * DIFF_W + 2 * MEM_W

kernel_name = 'hybrid_grid_encoder_block'


def rmsnorm(x, g):
    xf = x.astype(jnp.float32)
    y = xf * lax.rsqrt(jnp.mean(xf * xf, axis=-1, keepdims=True) + EPS)
    return (y * g.astype(jnp.float32)).astype(x.dtype)


def l2norm(x):
    xf = x.astype(jnp.float32)
    return (xf * lax.rsqrt(jnp.sum(xf * xf, axis=-1, keepdims=True) + EPS)).astype(x.dtype)


def rope(x, pos):
    d = x.shape[-1]
    inv = ROPE_THETA ** (-jnp.arange(0, d, 2, dtype=jnp.float32) / d)
    ang = pos.astype(jnp.float32)[:, None] * inv[None, :]
    cos, sin = jnp.cos(ang), jnp.sin(ang)
    x1, x2 = jnp.split(x.astype(jnp.float32), 2, axis=-1)
    return jnp.concatenate([x1 * cos - x2 * sin, x1 * sin + x2 * cos], axis=-1).astype(x.dtype)


def split_cols(t, sizes):
    out, start = [], 0
    for s in sizes:
        out.append(t[..., start:start + s])
        start += s
    return out


def to_heads(t, n_heads):
    B, T, _ = t.shape
    return t.reshape(B, T, n_heads, -1).transpose(0, 2, 1, 3)


def merge_heads(o):
    B, H, T, d = o.shape
    return o.transpose(0, 2, 1, 3).reshape(B, T, H * d)


def dwconv_centred(x, w):
    K, C = w.shape
    pad = K // 2
    return lax.conv_general_dilated(x, w[:, None, :], window_strides=(1,), padding=[(pad, pad)],
                                    dimension_numbers=('NWC', 'WIO', 'NWC'), feature_group_count=C)


def neighbourhood_attention(q, k, v, rpb):
    B, H, T, d = q.shape
    rows = T // GRID_W
    kh = min(NA_WIN_H, rows)
    r = jnp.arange(rows)
    r0 = jnp.clip(r - kh // 2, 0, rows - kh)
    key_rows = r0[:, None] + jnp.arange(kh)[None, :]
    c = jnp.arange(GRID_W)
    c0 = jnp.clip(c - NA_WIN_W // 2, 0, GRID_W - NA_WIN_W)
    col_ok = (c[None, :] >= c0[:, None]) & (c[None, :] < c0[:, None] + NA_WIN_W)
    qg = q.reshape(B, H, rows, GRID_W, d)
    kg = k.reshape(B, H, rows, GRID_W, d)[:, :, key_rows]
    vg = v.reshape(B, H, rows, GRID_W, d)[:, :, key_rows]
    s = jnp.einsum('bhrqd,bhrikd->bhrqik', qg, kg, preferred_element_type=jnp.float32) * (d ** -0.5)
    dr = key_rows - r[:, None] + (NA_WIN_H - 1)
    dc = jnp.clip(c[None, :] - c[:, None], -(NA_WIN_W - 1), NA_WIN_W - 1) + (NA_WIN_W - 1)
    bias = rpb[:, dr[:, None, :, None], dc[None, :, None, :]]
    s = s + bias[None].astype(jnp.float32)
    s = jnp.where(col_ok[None, None, None, :, None, :], s, -jnp.inf)
    p = jax.nn.softmax(s.reshape(B, H, rows, GRID_W, kh * GRID_W), axis=-1).reshape(s.shape)
    o = jnp.einsum('bhrqik,bhrikd->bhrqd', p.astype(v.dtype), vg)
    return o.reshape(B, H, T, d)


def gated_delta_rule_chunked(q, k, v, g, beta):
    out_dtype = v.dtype
    f32 = jnp.float32
    B, H, T, dk = q.shape
    dv = v.shape[-1]
    C = DN_CHUNK
    N = T // C
    q = q.astype(f32).reshape(B, H, N, C, dk)
    k = k.astype(f32).reshape(B, H, N, C, dk)
    v = v.astype(f32).reshape(B, H, N, C, dv)
    g = g.astype(f32).reshape(B, H, N, C)
    beta = beta.astype(f32).reshape(B, H, N, C)
    G = jnp.cumsum(g, axis=-1)
    lower_incl = jnp.tril(jnp.ones((C, C), bool))
    lower_strict = jnp.tril(jnp.ones((C, C), bool), -1)
    decay = jnp.exp(jnp.where(lower_incl, G[..., :, None] - G[..., None, :], -jnp.inf))
    kb = k * beta[..., None]
    a_mat = jnp.where(lower_strict, jnp.einsum('bhnid,bhnjd->bhnij', kb, k) * decay, 0.0)
    rhs = jnp.concatenate([v * beta[..., None], kb * jnp.exp(G)[..., None]], axis=-1)
    sol = lax.linalg.triangular_solve(a_mat + jnp.eye(C, dtype=f32), rhs, left_side=True, lower=True)
    u, w = sol[..., :dv], sol[..., dv:]
    qk = jnp.einsum('bhnid,bhnjd->bhnij', q, k) * decay
    q_dec = q * jnp.exp(G)[..., None]
    k_dec = k * jnp.exp(G[..., -1:] - G)[..., None]
    g_end = jnp.exp(G[..., -1])

    def step(S, xs):
        u_c, w_c, qk_c, qd_c, kd_c, ge = xs
        v_new = u_c - jnp.einsum('bhcd,bhde->bhce', w_c, S)
        o = jnp.einsum('bhcd,bhde->bhce', qd_c, S) + jnp.einsum('bhij,bhje->bhie', qk_c, v_new)
        S = S * ge[..., None, None] + jnp.einsum('bhcd,bhce->bhde', kd_c, v_new)
        return S, o

    xs = tuple(jnp.moveaxis(a, 2, 0) for a in (u, w, qk, q_dec, k_dec, g_end))
    S0 = jnp.zeros((B, H, dk, dv), f32)
    _, o = lax.scan(step, S0, xs)
    return jnp.moveaxis(o, 0, 2).reshape(B, H, T, dv).astype(out_dtype)


def deltanet_branch(qkv, ba, conv_w, a_log, dt_bias, out_norm):
    qkv = jax.nn.silu(dwconv_centred(qkv, conv_w))
    q, k, v = [to_heads(t, DN_HEADS) for t in jnp.split(qkv, 3, axis=-1)]
    q = l2norm(q) * (HEAD_DIM ** -0.5)
    k = l2norm(k)
    b_f, b_b, a_f, a_b = [t.transpose(0, 2, 1).astype(jnp.float32) for t in jnp.split(ba, 4, axis=-1)]
    al = a_log.astype(jnp.float32)
    dtb = dt_bias.astype(jnp.float32)
    g_f = -jnp.exp(al[0])[None, :, None] * jax.nn.softplus(a_f + dtb[0][None, :, None])
    g_b = -jnp.exp(al[1])[None, :, None] * jax.nn.softplus(a_b + dtb[1][None, :, None])
    o_f = gated_delta_rule_chunked(q, k, v, g_f, jax.nn.sigmoid(b_f))
    flip = lambda t: jnp.flip(t, axis=2)
    o_b = flip(gated_delta_rule_chunked(flip(q), flip(k), flip(v), flip(g_b), flip(jax.nn.sigmoid(b_b))))
    return merge_heads(rmsnorm(o_f + o_b, out_norm))


def memory_branch(mem_q, mem_n, w_kv, q_norm, k_norm):
    q = rmsnorm(to_heads(mem_q, MEM_HEADS), q_norm)
    k, v = jnp.split(mem_n @ w_kv, 2, axis=-1)
    k = rmsnorm(to_heads(k, MEM_HEADS), k_norm)
    v = to_heads(v, MEM_HEADS)
    s = jnp.einsum('bhtd,bhmd->bhtm', q, k, preferred_element_type=jnp.float32) * (HEAD_DIM ** -0.5)
    p = jax.nn.softmax(s, axis=-1)
    return merge_heads(jnp.einsum('bhtm,bhmd->bhtd', p.astype(v.dtype), v))


def diff_attention(q, k, v, lam):
    B, H, _, T, d = q.shape
    nb = T // Q_BLOCK
    qb = jnp.moveaxis(q.reshape(B, H, 2, nb, Q_BLOCK, d), 3, 0)

    def block(qi):
        s = jnp.einsum('bhmqd,bhmkd->bhmqk', qi, k, preferred_element_type=jnp.float32) * (d ** -0.5)
        p = jax.nn.softmax(s, axis=-1)
        attn = p[:, :, 0] - lam * p[:, :, 1]
        return jnp.einsum('bhqk,bhke->bhqe', attn.astype(v.dtype), v)

    o = lax.map(block, qb)
    return jnp.moveaxis(o, 0, 2).reshape(B, H, T, v.shape[-1])


def even_layer(x, mem_n, ln_g, w_in, w_mem_kv, w_out, na_q_norm, na_k_norm, na_rpb,
               dn_conv_w, dn_a_log, dn_dt_bias, dn_out_norm, mem_q_norm, mem_k_norm):
    h = rmsnorm(x, ln_g)
    proj = h @ w_in
    na_qkv, na_gate, dn_qkv, dn_gate, dn_ba, mem_q, mem_gate = split_cols(
        proj, [3 * NA_W, NA_W, 3 * DN_W, DN_W, 4 * DN_HEADS, MEM_W, MEM_W])
    q, k, v = [to_heads(t, NA_HEADS) for t in jnp.split(na_qkv, 3, axis=-1)]
    q = rmsnorm(q, na_q_norm)
    k = rmsnorm(k, na_k_norm)
    na_o = merge_heads(neighbourhood_attention(q, k, v, na_rpb)) * jax.nn.silu(na_gate)
    dn_o = deltanet_branch(dn_qkv, dn_ba, dn_conv_w, dn_a_log, dn_dt_bias, dn_out_norm) * jax.nn.silu(dn_gate)
    mem_o = memory_branch(mem_q, mem_n, w_mem_kv, mem_q_norm, mem_k_norm) * jax.nn.silu(mem_gate)
    return x + jnp.concatenate([na_o, dn_o, mem_o], axis=-1) @ w_out


def odd_layer(x, mem_n, layer_idx, ln_g, w_in, w_mem_kv, w_out, q_norm, k_norm, lam_vecs, subln_g,
              mem_q_norm, mem_k_norm):
    B, T, _ = x.shape
    h = rmsnorm(x, ln_g)
    proj = h @ w_in
    q, k, v, gate, mem_q, mem_gate = split_cols(proj, [DIFF_W, DIFF_W, DIFF_W, DIFF_W, MEM_W, MEM_W])
    pos = jnp.arange(T)
    q = q.reshape(B, T, DIFF_HEADS, 2, HEAD_DIM).transpose(0, 2, 3, 1, 4)
    k = k.reshape(B, T, DIFF_HEADS, 2, HEAD_DIM).transpose(0, 2, 3, 1, 4)
    q = rope(rmsnorm(q, q_norm), pos)
    k = rope(rmsnorm(k, k_norm), pos)
    v = to_heads(v, DIFF_HEADS)
    lam_init = 0.8 - 0.6 * math.exp(-0.3 * layer_idx)
    lv = lam_vecs.astype(jnp.float32)
    lam = jnp.exp(jnp.sum(lv[0] * lv[1])) - jnp.exp(jnp.sum(lv[2] * lv[3])) + lam_init
    o = rmsnorm(diff_attention(q, k, v, lam), subln_g) * (1.0 - lam_init)
    diff_o = merge_heads(o) * jax.nn.silu(gate)
    mem_o = memory_branch(mem_q, mem_n, w_mem_kv, mem_q_norm, mem_k_norm) * jax.nn.silu(mem_gate)
    return x + jnp.concatenate([diff_o, mem_o], axis=-1) @ w_out


def setup_inputs(seed: int = 0) -> dict:
    key = jax.random.key(seed)
    keys = iter(jax.random.split(key, 40))

    def nrm(shape, scale):
        return jax.random.normal(next(keys), shape, jnp.float32) * scale

    def gain(shape):
        return 1.0 + nrm(shape, 0.05)

    D = D_MODEL
    return {
        'x': nrm((BATCH, SEQ, D), 1.0),
        'mem': nrm((BATCH, MEM_LEN, D), 1.0),
        'mem_norm_g': gain((D,)),
        'e_ln_g': gain((N_EVEN, D)),
        'e_w_in': nrm((N_EVEN, D, EVEN_IN), D ** -0.5),
        'e_w_mem_kv': nrm((N_EVEN, D, 2 * MEM_W), D ** -0.5),
        'e_w_out': nrm((N_EVEN, MIX_W, D), MIX_W ** -0.5),
        'na_q_norm': gain((N_EVEN, HEAD_DIM)),
        'na_k_norm': gain((N_EVEN, HEAD_DIM)),
        'na_rpb': nrm((N_EVEN, NA_HEADS, 2 * NA_WIN_H - 1, 2 * NA_WIN_W - 1), 0.1),
        'dn_conv_w': nrm((N_EVEN, DN_CONV, 3 * DN_W), DN_CONV ** -0.5),
        'dn_a_log': jnp.log(jax.random.uniform(next(keys), (N_EVEN, 2, DN_HEADS), jnp.float32, 1.0, 16.0)),
        'dn_dt_bias': nrm((N_EVEN, 2, DN_HEADS), 0.1),
        'dn_out_norm': gain((N_EVEN, HEAD_DIM)),
        'e_mem_q_norm': gain((N_EVEN, HEAD_DIM)),
        'e_mem_k_norm': gain((N_EVEN, HEAD_DIM)),
        'o_ln_g': gain((N_ODD, D)),
        'o_w_in': nrm((N_ODD, D, ODD_IN), D ** -0.5),
        'o_w_mem_kv': nrm((N_ODD, D, 2 * MEM_W), D ** -0.5),
        'o_w_out': nrm((N_ODD, MIX_W, D), MIX_W ** -0.5),
        'df_q_norm': gain((N_ODD, HEAD_DIM)),
        'df_k_norm': gain((N_ODD, HEAD_DIM)),
        'df_lambda': nrm((N_ODD, 4, HEAD_DIM), 0.1),
        'df_subln': gain((N_ODD, DIFF_VDIM)),
        'o_mem_q_norm': gain((N_ODD, HEAD_DIM)),
        'o_mem_k_norm': gain((N_ODD, HEAD_DIM)),
    }


def reference(x, mem, mem_norm_g, e_ln_g, e_w_in, e_w_mem_kv, e_w_out, na_q_norm, na_k_norm, na_rpb,
              dn_conv_w, dn_a_log, dn_dt_bias, dn_out_norm, e_mem_q_norm, e_mem_k_norm,
              o_ln_g, o_w_in, o_w_mem_kv, o_w_out, df_q_norm, df_k_norm, df_lambda, df_subln,
              o_mem_q_norm, o_mem_k_norm):
    mem_n = rmsnorm(mem, mem_norm_g)
    for layer in range(DEPTH):
        i = layer // 2
        if layer % 2 == 0:
            x = even_layer(x, mem_n, e_ln_g[i], e_w_in[i], e_w_mem_kv[i], e_w_out[i], na_q_norm[i], na_k_norm[i],
                           na_rpb[i], dn_conv_w[i], dn_a_log[i], dn_dt_bias[i], dn_out_norm[i],
                           e_mem_q_norm[i], e_mem_k_norm[i])
        else:
            x = odd_layer(x, mem_n, layer, o_ln_g[i], o_w_in[i], o_w_mem_kv[i], o_w_out[i], df_q_norm[i],
                          df_k_norm[i], df_lambda[i], df_subln[i], o_mem_q_norm[i], o_mem_k_norm[i])
    return x
```

```python
import functools
import math

import numpy as np
import jax
import jax.numpy as jnp
from jax import lax
from jax.experimental import pallas as pl
from jax.experimental.pallas import tpu as pltpu

F32 = jnp.float32
BF16 = jnp.bfloat16

HEAD_DIM = 128
GRID_W = 64
NA_HEADS = 6
NA_WIN_H = 8
NA_WIN_W = 16
DN_HEADS = 6
DN_CONV = 5
DN_CHUNK = 64
MEM_HEADS = 4
DIFF_HEADS = 6
DIFF_VDIM = 2 * HEAD_DIM
ROPE_THETA = 10000.0
EPS = 1e-6

NA_W = NA_HEADS * HEAD_DIM
DN_W = DN_HEADS * HEAD_DIM
MEM_W = MEM_HEADS * HEAD_DIM
DIFF_W = DIFF_HEADS * DIFF_VDIM

LANES = 128
VMEM_LIMIT = 56 * 1024 * 1024

NA_QROWS = 4
NA_KROWS = 12
NA_TQ = NA_QROWS * GRID_W
NA_TK = NA_KROWS * GRID_W


def _cparams(*sem):
    return pltpu.CompilerParams(dimension_semantics=sem, vmem_limit_bytes=VMEM_LIMIT)


def _dot(a, b):
    return jnp.dot(a, b, preferred_element_type=F32)


def _dot_nt(a, b):
    return lax.dot_general(a, b, (((1,), (1,)), ((), ())), preferred_element_type=F32)


def _silu(x):
    return x * (1.0 / (1.0 + jnp.exp(-x)))


def _rms(x, g):
    return x * lax.rsqrt(jnp.mean(x * x, axis=-1, keepdims=True) + EPS) * g


def _proj_kernel(x_ref, g_ref, w_ref, *rest, has_aux):
    if has_aux:
        waux_ref, o_ref, aux_ref, h_ref = rest
    else:
        o_ref, h_ref = rest

    @pl.when(pl.program_id(1) == 0)
    def _():
        h = _rms(x_ref[...], g_ref[...]).astype(BF16)
        h_ref[...] = h
        if has_aux:
            aux_ref[...] = _dot(h, waux_ref[...])

    o_ref[...] = _dot(h_ref[...], w_ref[...]).astype(o_ref.dtype)


def _proj(x2d, g, w, w_aux=None, *, tm, tn, name):
    n, d = x2d.shape
    nc = w.shape[1]
    has_aux = w_aux is not None
    in_specs = [
        pl.BlockSpec((tm, d), lambda i, j: (i, 0)),
        pl.BlockSpec((1, d), lambda i, j: (0, 0)),
        pl.BlockSpec((d, tn), lambda i, j: (0, j)),
    ]
    out_shape = [jax.ShapeDtypeStruct((n, nc), BF16)]
    out_specs = [pl.BlockSpec((tm, tn), lambda i, j: (i, j))]
    args = [x2d, g.reshape(1, d), w]
    if has_aux:
        in_specs.append(pl.BlockSpec((d, LANES), lambda i, j: (0, 0)))
        out_shape.append(jax.ShapeDtypeStruct((n, LANES), F32))
        out_specs.append(pl.BlockSpec((tm, LANES), lambda i, j: (i, 0)))
        args.append(w_aux)
    res = pl.pallas_call(
        functools.partial(_proj_kernel, has_aux=has_aux),
        grid=(n // tm, nc // tn),
        in_specs=in_specs,
        out_specs=out_specs,
        out_shape=out_shape,
        scratch_shapes=[pltpu.VMEM((tm, d), BF16)],
        compiler_params=_cparams("parallel", "arbitrary"),
        name=name,
    )(*args)
    return res if has_aux else res[0]


def _outproj_kernel(*refs, n_in):
    x_ref = refs[0]
    a_refs = refs[1:1 + n_in]
    w_refs = refs[1 + n_in:1 + 2 * n_in]
    o_ref = refs[1 + 2 * n_in]
    acc = x_ref[...]
    for a_ref, w_ref in zip(a_refs, w_refs):
        acc = acc + _dot(a_ref[...], w_ref[...])
    o_ref[...] = acc


def _outproj(x2d, acts, weights, *, tm, name):
    n, d = x2d.shape
    n_in = len(acts)
    in_specs = [pl.BlockSpec((tm, d), lambda i: (i, 0))]
    in_specs += [pl.BlockSpec((tm, a.shape[1]), lambda i: (i, 0)) for a in acts]
    in_specs += [pl.BlockSpec(w.shape, lambda i: (0, 0)) for w in weights]
    return pl.pallas_call(
        functools.partial(_outproj_kernel, n_in=n_in),
        grid=(n // tm,),
        in_specs=in_specs,
        out_specs=pl.BlockSpec((tm, d), lambda i: (i, 0)),
        out_shape=jax.ShapeDtypeStruct((n, d), F32),
        compiler_params=_cparams("parallel"),
        name=name,
    )(x2d, *acts, *weights)


def _mem_kernel(q_ref, gate_ref, k_ref, v_ref, qn_ref, kn_ref, o_ref):
    scale = HEAD_DIM ** -0.5
    for h in range(MEM_HEADS):
        cols = slice(h * HEAD_DIM, (h + 1) * HEAD_DIM)
        q = (_rms(q_ref[:, cols].astype(F32), qn_ref[...]) * scale).astype(BF16)
        k = _rms(k_ref[:, cols].astype(F32), kn_ref[...]).astype(BF16)
        s = _dot_nt(q, k)
        e = jnp.exp(s - jnp.max(s, axis=-1, keepdims=True))
        p = (e * (1.0 / jnp.sum(e, axis=-1, keepdims=True))).astype(BF16)
        o = _dot(p, v_ref[:, cols])
        o_ref[:, cols] = (o * _silu(gate_ref[:, cols].astype(F32))).astype(o_ref.dtype)


def _mem_attention(proj, q_blk, gate_blk, kv, q_norm, k_norm, *, batch, tq, name):
    n = proj.shape[0]
    t = n // batch
    m = kv.shape[0] // batch
    steps = t // tq
    return pl.pallas_call(
        _mem_kernel,
        grid=(batch, steps),
        in_specs=[
            pl.BlockSpec((tq, MEM_W), lambda b, i: (b * steps + i, q_blk)),
            pl.BlockSpec((tq, MEM_W), lambda b, i: (b * steps + i, gate_blk)),
            pl.BlockSpec((m, MEM_W), lambda b, i: (b, 0)),
            pl.BlockSpec((m, MEM_W), lambda b, i: (b, 1)),
            pl.BlockSpec((1, HEAD_DIM), lambda b, i: (0, 0)),
            pl.BlockSpec((1, HEAD_DIM), lambda b, i: (0, 0)),
        ],
        out_specs=pl.BlockSpec((tq, MEM_W), lambda b, i: (b * steps + i, 0)),
        out_shape=jax.ShapeDtypeStruct((n, MEM_W), BF16),
        compiler_params=_cparams("parallel", "parallel"),
        name=name,
    )(proj, proj, kv, kv, q_norm.reshape(1, HEAD_DIM), k_norm.reshape(1, HEAD_DIM))


def _na_bias_kernel(rpb_ref, o_ref, *, n_rows):
    h = pl.program_id(0)
    n_dc = 2 * NA_WIN_W - 1
    qc = lax.broadcasted_iota(jnp.int32, (GRID_W, GRID_W), 0)
    kc = lax.broadcasted_iota(jnp.int32, (GRID_W, GRID_W), 1)
    dcm = jnp.clip(kc - qc, -(NA_WIN_W - 1), NA_WIN_W - 1) + (NA_WIN_W - 1)
    c0 = jnp.clip(qc - NA_WIN_W // 2, 0, GRID_W - NA_WIN_W)
    col_ok = (kc >= c0) & (kc < c0 + NA_WIN_W)
    neg = jnp.full((GRID_W, GRID_W), -jnp.inf, F32)
    tiles = []
    for dr in range(2 * NA_WIN_H - 1):
        t = jnp.zeros((GRID_W, GRID_W), F32)
        for dc in range(n_dc):
            t = jnp.where(dcm == dc, rpb_ref[h, dr * n_dc + dc], t)
        tiles.append(jnp.where(col_ok, t, neg))
    n_blocks = n_rows // NA_QROWS
    for case, j in enumerate((0, 1, n_blocks - 1)):
        start = min(max(NA_QROWS * j - NA_QROWS, 0), n_rows - NA_KROWS)
        for a in range(NA_QROWS):
            qr = NA_QROWS * j + a
            r0 = min(max(qr - NA_WIN_H // 2, 0), n_rows - NA_WIN_H)
            for b in range(NA_KROWS):
                kr = start + b
                tile = tiles[kr - qr + NA_WIN_H - 1] if r0 <= kr < r0 + NA_WIN_H else neg
                o_ref[0, case, a * GRID_W:(a + 1) * GRID_W, b * GRID_W:(b + 1) * GRID_W] = tile


def _na_bias(rpb, n_rows):
    h = rpb.shape[0]
    return pl.pallas_call(
        functools.partial(_na_bias_kernel, n_rows=n_rows),
        grid=(h,),
        in_specs=[pl.BlockSpec(memory_space=pltpu.SMEM)],
        out_specs=pl.BlockSpec((1, 3, NA_TQ, NA_TK), lambda i: (i, 0, 0, 0)),
        out_shape=jax.ShapeDtypeStruct((h, 3, NA_TQ, NA_TK), F32),
        compiler_params=_cparams("parallel"),
        name="na_bias",
    )(rpb.reshape(h, -1))


def _na_kernel(q_ref, k_ref, v_ref, gate_ref, bias_ref, qn_ref, kn_ref, o_ref, *, n_blocks):
    j = pl.program_id(2)
    start = pl.multiple_of(jnp.clip(j - 1, 0, n_blocks - NA_KROWS // NA_QROWS) * NA_TQ, NA_TQ)
    q = (_rms(q_ref[...].astype(F32), qn_ref[...]) * HEAD_DIM ** -0.5).astype(BF16)
    k = _rms(k_ref[pl.ds(start, NA_TK), :].astype(F32), kn_ref[...]).astype(BF16)
    s = _dot_nt(q, k) + bias_ref[0, 0]
    e = jnp.exp(s - jnp.max(s, axis=-1, keepdims=True))
    p = (e * (1.0 / jnp.sum(e, axis=-1, keepdims=True))).astype(BF16)
    o = _dot(p, v_ref[pl.ds(start, NA_TK), :])
    o_ref[...] = (o * _silu(gate_ref[...].astype(F32))).astype(o_ref.dtype)


def _na_attention(proj, bias, q_norm, k_norm, *, batch):
    n = proj.shape[0]
    t = n // batch
    n_blocks = t // NA_TQ
    last = n_blocks - 1

    def case(j):
        return jnp.where(j == 0, 0, jnp.where(j == last, 2, 1))

    return pl.pallas_call(
        functools.partial(_na_kernel, n_blocks=n_blocks),
        grid=(batch, NA_HEADS, n_blocks),
        in_specs=[
            pl.BlockSpec((NA_TQ, HEAD_DIM), lambda b, h, j: (b * n_blocks + j, h)),
            pl.BlockSpec((t, HEAD_DIM), lambda b, h, j: (b, NA_HEADS + h)),
            pl.BlockSpec((t, HEAD_DIM), lambda b, h, j: (b, 2 * NA_HEADS + h)),
            pl.BlockSpec((NA_TQ, HEAD_DIM), lambda b, h, j: (b * n_blocks + j, 3 * NA_HEADS + h)),
            pl.BlockSpec((1, 1, NA_TQ, NA_TK), lambda b, h, j: (h, case(j), 0, 0)),
            pl.BlockSpec((1, HEAD_DIM), lambda b, h, j: (0, 0)),
            pl.BlockSpec((1, HEAD_DIM), lambda b, h, j: (0, 0)),
        ],
        out_specs=pl.BlockSpec((NA_TQ, HEAD_DIM), lambda b, h, j: (b * n_blocks + j, h)),
        out_shape=jax.ShapeDtypeStruct((n, NA_W), BF16),
        compiler_params=_cparams("parallel", "parallel", "arbitrary"),
        name="na_attention",
    )(proj, proj, proj, proj, bias, q_norm.reshape(1, HEAD_DIM), k_norm.reshape(1, HEAD_DIM))


def _split3(x):
    hi = x.astype(BF16)
    r = x - hi.astype(F32)
    mid = r.astype(BF16)
    lo = (r - mid.astype(F32)).astype(BF16)
    return hi, mid, lo


def _dn_gates_kernel(ba_ref, alog_ref, dtb_ref, col_ref, row_ref):
    tm = ba_ref.shape[0]
    ba = ba_ref[...]
    lane = lax.broadcasted_iota(jnp.int32, (1, LANES), 1)
    beta = 1.0 / (1.0 + jnp.exp(-ba))
    z = ba + dtb_ref[...]
    softplus = jnp.maximum(z, 0.0) + jnp.log(1.0 + jnp.exp(-jnp.abs(z)))
    g = -jnp.exp(alog_ref[...]) * softplus
    r = lax.broadcasted_iota(jnp.int32, (tm, tm), 0)
    c = lax.broadcasted_iota(jnp.int32, (tm, tm), 1)
    same = (r // DN_CHUNK) == (c // DN_CHUNK)
    m_f = jnp.where(same & (c <= r), 1.0, 0.0).astype(BF16)
    m_b = jnp.where(same & (c >= r), 1.0, 0.0).astype(BF16)
    parts = _split3(g)
    g_f = _dot(m_f, parts[0]) + _dot(m_f, parts[1]) + _dot(m_f, parts[2])
    g_b = _dot(m_b, parts[0]) + _dot(m_b, parts[1]) + _dot(m_b, parts[2])
    nh = DN_HEADS
    out = jnp.where(lane < 2 * nh, beta, jnp.where(lane < 3 * nh, g_f, jnp.where(lane < 4 * nh, g_b, 0.0)))
    col_ref[...] = out
    row_ref[0] = out.T


def _dn_gates(ba, a_log, dt_bias, *, batch, tm=512):
    n = ba.shape[0]
    t = n // batch
    steps = t // tm
    pad = jnp.zeros((2 * DN_HEADS,), F32)
    alog_row = jnp.concatenate([pad, a_log.reshape(-1).astype(F32), jnp.zeros((LANES - 4 * DN_HEADS,), F32)])
    dtb_row = jnp.concatenate([pad, dt_bias.reshape(-1).astype(F32), jnp.zeros((LANES - 4 * DN_HEADS,), F32)])
    return pl.pallas_call(
        _dn_gates_kernel,
        grid=(batch, steps),
        in_specs=[
            pl.BlockSpec((tm, LANES), lambda b, i: (b * steps + i, 0)),
            pl.BlockSpec((1, LANES), lambda b, i: (0, 0)),
            pl.BlockSpec((1, LANES), lambda b, i: (0, 0)),
        ],
        out_specs=[
            pl.BlockSpec((tm, LANES), lambda b, i: (b * steps + i, 0)),
            pl.BlockSpec((1, LANES, tm), lambda b, i: (b, 0, i)),
        ],
        out_shape=[
            jax.ShapeDtypeStruct((n, LANES), F32),
            jax.ShapeDtypeStruct((batch, LANES, t), F32),
        ],
        compiler_params=_cparams("parallel", "parallel"),
        name="dn_gates",
    )(ba, alog_row.reshape(1, LANES), dtb_row.reshape(1, LANES))


def _dn_kernel(q_ref, k_ref, v_ref, gate_ref, cwq_ref, cwk_ref, cwv_ref, gc_ref, grf_ref, grb_ref,
               on_ref, o_ref, qn_s, kn_s, vn_s, u_s, wq_s, l2_s, of_s, ob_s):
    t = q_ref.shape[0]
    c_sz = DN_CHUNK
    n_chunks = t // c_sz
    head = pl.program_id(1)

    row = lax.broadcasted_iota(jnp.int32, (t, 1), 0)

    def conv_silu(x_ref, cw_ref):
        x = x_ref[...].astype(F32)
        acc = x * cw_ref[DN_CONV // 2:DN_CONV // 2 + 1, :]
        for tap in range(DN_CONV):
            off = tap - DN_CONV // 2
            if off == 0:
                continue
            shifted = pltpu.roll(x, (-off) % t, 0)
            ok = (row + off >= 0) & (row + off < t)
            acc = acc + jnp.where(ok, shifted, 0.0) * cw_ref[tap:tap + 1, :]
        return _silu(acc)

    def l2n(x):
        return x * lax.rsqrt(jnp.sum(x * x, axis=-1, keepdims=True) + EPS)

    qn_s[...] = l2n(conv_silu(q_ref, cwq_ref)) * HEAD_DIM ** -0.5
    kn_s[...] = l2n(conv_silu(k_ref, cwk_ref))
    vn_s[...] = conv_silu(v_ref, cwv_ref)

    lane = lax.broadcasted_iota(jnp.int32, (1, LANES), 1)
    ii = lax.broadcasted_iota(jnp.int32, (c_sz, c_sz), 0)
    jj = lax.broadcasted_iota(jnp.int32, (c_sz, c_sz), 1)
    eye = jnp.where(ii == jj, 1.0, 0.0)

    def local_body(c, carry):
        rows = pl.ds(pl.multiple_of(c * c_sz, c_sz), c_sz)
        kc = kn_s[rows, :]
        qc = qn_s[rows, :]
        vc = vn_s[rows, :]
        gcb = gc_ref[rows, :]

        def col(idx):
            return jnp.sum(jnp.where(lane == idx, gcb, 0.0), axis=1, keepdims=True)

        k16 = kc.astype(BF16)
        kk = _dot_nt(k16, k16)
        qk_raw = _dot_nt(qc.astype(BF16), k16)
        for d, gr_ref in enumerate((grf_ref, grb_ref)):
            beta = col(d * DN_HEADS + head)
            g_col = col((2 + d) * DN_HEADS + head)
            g_row = gr_ref[0, 0, pl.ds(c, 1), :]
            incl = (ii >= jj) if d == 0 else (ii <= jj)
            strict = (ii > jj) if d == 0 else (ii < jj)
            g_edge = g_row[:, c_sz - 1:c_sz] if d == 0 else g_row[:, 0:1]
            decay = jnp.where(incl, jnp.exp(jnp.where(incl, g_col - g_row, 0.0)), 0.0)
            a = jnp.where(strict, beta * kk * decay, 0.0)
            p = eye - a
            x = a
            for _ in range(int(math.log2(c_sz)) - 1):
                x16 = x.astype(BF16)
                x = _dot(x16, x16)
                p = p + _dot(p.astype(BF16), x.astype(BF16))
            e_g = jnp.exp(g_col)
            rhs = jnp.concatenate([vc * beta, kc * (beta * e_g)], axis=1).astype(BF16)
            sol = _dot(p.astype(BF16), rhs)
            u_s[d, rows, :] = sol[:, :HEAD_DIM]
            wq_s[d, c, 0:c_sz, :] = sol[:, HEAD_DIM:].astype(BF16)
            wq_s[d, c, c_sz:2 * c_sz, :] = (qc * e_g).astype(BF16)
            l2_s[d, c, 0:c_sz, :] = (qk_raw * decay).astype(BF16)
            kd = kc * jnp.exp(g_edge - g_col)
            l2_s[d, c, c_sz:c_sz + HEAD_DIM, :] = kd.T.astype(BF16)
        return carry

    lax.fori_loop(0, n_chunks, local_body, 0)

    def scan_body(i, states):
        new_states = []
        for d, (gr_ref, o_s) in enumerate(((grf_ref, of_s), (grb_ref, ob_s))):
            c = i if d == 0 else n_chunks - 1 - i
            rows = pl.ds(pl.multiple_of(c * c_sz, c_sz), c_sz)
            s = states[d]
            g_row = gr_ref[0, 0, pl.ds(c, 1), :]
            g_edge = g_row[:, c_sz - 1:c_sz] if d == 0 else g_row[:, 0:1]
            ws_qs = _dot(wq_s[d, c], s.astype(BF16))
            v_new = u_s[d, rows, :] - ws_qs[:c_sz]
            both = _dot(l2_s[d, c], v_new.astype(BF16))
            o_s[rows, :] = ws_qs[c_sz:] + both[:c_sz]
            new_states.append(s * jnp.exp(g_edge) + both[c_sz:])
        return tuple(new_states)

    zero = jnp.zeros((HEAD_DIM, HEAD_DIM), F32)
    lax.fori_loop(0, n_chunks, scan_body, (zero, zero))

    o = _rms(of_s[...] + ob_s[...], on_ref[...])
    o_ref[...] = (o * _silu(gate_ref[...].astype(F32))).astype(o_ref.dtype)


def _deltanet(proj, conv_w, gates_col, gates_row, out_norm, *, batch, col0):
    n = proj.shape[0]
    t = n // batch
    n_chunks = t // DN_CHUNK
    nh = DN_HEADS
    gr = gates_row.reshape(batch, LANES, n_chunks, DN_CHUNK)
    return pl.pallas_call(
        _dn_kernel,
        grid=(batch, nh),
        in_specs=[
            pl.BlockSpec((t, HEAD_DIM), lambda b, h: (b, col0 + h)),
            pl.BlockSpec((t, HEAD_DIM), lambda b, h: (b, col0 + nh + h)),
            pl.BlockSpec((t, HEAD_DIM), lambda b, h: (b, col0 + 2 * nh + h)),
            pl.BlockSpec((t, HEAD_DIM), lambda b, h: (b, col0 + 3 * nh + h)),
            pl.BlockSpec((DN_CONV, HEAD_DIM), lambda b, h: (0, h)),
            pl.BlockSpec((DN_CONV, HEAD_DIM), lambda b, h: (0, nh + h)),
            pl.BlockSpec((DN_CONV, HEAD_DIM), lambda b, h: (0, 2 * nh + h)),
            pl.BlockSpec((t, LANES), lambda b, h: (b, 0)),
            pl.BlockSpec((1, 1, n_chunks, DN_CHUNK), lambda b, h: (b, 2 * nh + h, 0, 0)),
            pl.BlockSpec((1, 1, n_chunks, DN_CHUNK), lambda b, h: (b, 3 * nh + h, 0, 0)),
            pl.BlockSpec((1, HEAD_DIM), lambda b, h: (0, 0)),
        ],
        out_specs=pl.BlockSpec((t, HEAD_DIM), lambda b, h: (b, h)),
        out_shape=jax.ShapeDtypeStruct((n, DN_W), BF16),
        scratch_shapes=[
            pltpu.VMEM((t, HEAD_DIM), F32),
            pltpu.VMEM((t, HEAD_DIM), F32),
            pltpu.VMEM((t, HEAD_DIM), F32),
            pltpu.VMEM((2, t, HEAD_DIM), F32),
            pltpu.VMEM((2, n_chunks, 2 * DN_CHUNK, HEAD_DIM), BF16),
            pltpu.VMEM((2, n_chunks, DN_CHUNK + HEAD_DIM, DN_CHUNK), BF16),
            pltpu.VMEM((t, HEAD_DIM), F32),
            pltpu.VMEM((t, HEAD_DIM), F32),
        ],
        compiler_params=_cparams("parallel", "parallel"),
        name="deltanet",
    )(proj, proj, proj, proj, conv_w, conv_w, conv_w, gates_col, gr, gr, out_norm.reshape(1, HEAD_DIM))


def _rope_tables(t):
    inv = ROPE_THETA ** (-np.arange(0, HEAD_DIM, 2, dtype=np.float64) / HEAD_DIM)
    ang = (np.arange(t, dtype=np.float32)[:, None] * inv.astype(np.float32)[None, :]).astype(np.float64)
    cos, sin = np.cos(ang), np.sin(ang)
    cos_full = np.concatenate([cos, cos], axis=1).astype(np.float32)
    sin_signed = np.concatenate([-sin, sin], axis=1).astype(np.float32)
    return jnp.asarray(cos_full), jnp.asarray(sin_signed)


def _rope(x, cos, sin_signed):
    return x * cos + pltpu.roll(x, HEAD_DIM // 2, 1) * sin_signed


def _diff_kernel(q_ref, k_ref, v_ref, gate_ref, cosq_ref, sinq_ref, cosk_ref, sink_ref, qn_ref, kn_ref,
                 lam_ref, sub_ref, o_ref, k_s, *, lam_init):
    d = HEAD_DIM

    @pl.when(pl.program_id(2) == 0)
    def _():
        for m in range(2):
            k = _rms(k_ref[:, m * d:(m + 1) * d].astype(F32), kn_ref[...])
            k_s[m] = _rope(k, cosk_ref[...], sink_ref[...]).astype(BF16)

    lv = lam_ref[...]
    lam = (jnp.exp(jnp.sum(lv[0:1] * lv[1:2], axis=-1, keepdims=True))
           - jnp.exp(jnp.sum(lv[2:3] * lv[3:4], axis=-1, keepdims=True)) + lam_init)
    probs = []
    for m in range(2):
        q = _rms(q_ref[:, m * d:(m + 1) * d].astype(F32), qn_ref[...])
        q = (_rope(q, cosq_ref[...], sinq_ref[...]) * d ** -0.5).astype(BF16)
        s = _dot_nt(q, k_s[m])
        e = jnp.exp(s - jnp.max(s, axis=-1, keepdims=True))
        probs.append(e * (1.0 / jnp.sum(e, axis=-1, keepdims=True)))
    attn = (probs[0] - lam * probs[1]).astype(BF16)
    o = _dot(attn, v_ref[...])
    o = _rms(o, sub_ref[...]) * (1.0 - lam_init)
    o_ref[...] = (o * _silu(gate_ref[...].astype(F32))).astype(o_ref.dtype)


def _diff_attention(proj, q_norm, k_norm, lam_vecs, subln, *, batch, layer_idx, tq):
    n = proj.shape[0]
    t = n // batch
    steps = t // tq
    nh = DIFF_HEADS
    w = DIFF_VDIM
    cos, sin = _rope_tables(t)
    lam_init = 0.8 - 0.6 * math.exp(-0.3 * layer_idx)
    const = lambda shape: pl.BlockSpec(shape, lambda b, h, i: (0, 0))
    return pl.pallas_call(
        functools.partial(_diff_kernel, lam_init=lam_init),
        grid=(batch, nh, steps),
        in_specs=[
            pl.BlockSpec((tq, w), lambda b, h, i: (b * steps + i, h)),
            pl.BlockSpec((t, w), lambda b, h, i: (b, nh + h)),
            pl.BlockSpec((t, w), lambda b, h, i: (b, 2 * nh + h)),
            pl.BlockSpec((tq, w), lambda b, h, i: (b * steps + i, 3 * nh + h)),
            pl.BlockSpec((tq, HEAD_DIM), lambda b, h, i: (i, 0)),
            pl.BlockSpec((tq, HEAD_DIM), lambda b, h, i: (i, 0)),
            const((t, HEAD_DIM)),
            const((t, HEAD_DIM)),
            const((1, HEAD_DIM)),
            const((1, HEAD_DIM)),
            const((4, HEAD_DIM)),
            const((1, w)),
        ],
        out_specs=pl.BlockSpec((tq, w), lambda b, h, i: (b * steps + i, h)),
        out_shape=jax.ShapeDtypeStruct((n, DIFF_W), BF16),
        scratch_shapes=[pltpu.VMEM((2, t, HEAD_DIM), BF16)],
        compiler_params=_cparams("parallel", "parallel", "arbitrary"),
        name="diff_attention",
    )(proj, proj, proj, proj, cos, sin, cos, sin, q_norm.reshape(1, HEAD_DIM), k_norm.reshape(1, HEAD_DIM),
      lam_vecs.astype(F32), subln.reshape(1, w))


def _even_layer(x2d, mem2d, mem_g, ln_g, w_in, w_mem_kv, w_out, na_q_norm, na_k_norm, na_rpb, dn_conv_w,
                dn_a_log, dn_dt_bias, dn_out_norm, mem_q_norm, mem_k_norm, *, batch):
    t = x2d.shape[0] // batch
    ba0 = 4 * NA_W + 4 * DN_W
    ba1 = ba0 + 4 * DN_HEADS
    w_main = jnp.concatenate([w_in[:, :ba0], w_in[:, ba1:]], axis=1).astype(BF16)
    w_ba = jnp.pad(w_in[:, ba0:ba1], ((0, 0), (0, LANES - 4 * DN_HEADS))).astype(BF16)
    proj, ba = _proj(x2d, ln_g, w_main, w_ba, tm=1024, tn=1024, name="even_in_proj")
    kv = _proj(mem2d, mem_g, w_mem_kv.astype(BF16), tm=mem2d.shape[0], tn=2 * MEM_W, name="even_mem_kv")

    bias = _na_bias(na_rpb.astype(F32), t // GRID_W)
    na_o = _na_attention(proj, bias, na_q_norm, na_k_norm, batch=batch)
    g_col, g_row = _dn_gates(ba, dn_a_log, dn_dt_bias, batch=batch)
    dn_o = _deltanet(proj, dn_conv_w.astype(F32), g_col, g_row, dn_out_norm, batch=batch,
                     col0=4 * NA_W // HEAD_DIM)
    mem_blk = ba0 // MEM_W
    mem_o = _mem_attention(proj, mem_blk, mem_blk + 1, kv, mem_q_norm, mem_k_norm, batch=batch, tq=1024,
                           name="even_mem_attention")
    w_o = w_out.astype(BF16)
    return _outproj(x2d, [na_o, dn_o, mem_o],
                    [w_o[:NA_W], w_o[NA_W:NA_W + DN_W], w_o[NA_W + DN_W:]], tm=512, name="even_out_proj")


def _odd_layer(x2d, mem2d, mem_g, layer_idx, ln_g, w_in, w_mem_kv, w_out, q_norm, k_norm, lam_vecs, subln_g,
               mem_q_norm, mem_k_norm, *, batch):
    proj = _proj(x2d, ln_g, w_in.astype(BF16), tm=1024, tn=1024, name="odd_in_proj")
    kv = _proj(mem2d, mem_g, w_mem_kv.astype(BF16), tm=mem2d.shape[0], tn=2 * MEM_W, name="odd_mem_kv")
    diff_o = _diff_attention(proj, q_norm, k_norm, lam_vecs, subln_g, batch=batch, layer_idx=layer_idx, tq=256)
    mem_blk = 4 * DIFF_W // MEM_W
    mem_o = _mem_attention(proj, mem_blk, mem_blk + 1, kv, mem_q_norm, mem_k_norm, batch=batch, tq=1024,
                           name="odd_mem_attention")
    w_o = w_out.astype(BF16)
    return _outproj(x2d, [diff_o, mem_o], [w_o[:DIFF_W], w_o[DIFF_W:]], tm=512, name="odd_out_proj")


def kernel(x, mem, mem_norm_g, e_ln_g, e_w_in, e_w_mem_kv, e_w_out, na_q_norm, na_k_norm, na_rpb, dn_conv_w,
           dn_a_log, dn_dt_bias, dn_out_norm, e_mem_q_norm, e_mem_k_norm, o_ln_g, o_w_in, o_w_mem_kv, o_w_out,
           df_q_norm, df_k_norm, df_lambda, df_subln, o_mem_q_norm, o_mem_k_norm):
    batch, t, d = x.shape
    depth = e_ln_g.shape[0] + o_ln_g.shape[0]
    x2d = x.reshape(batch * t, d)
    mem2d = mem.reshape(-1, d)
    for layer in range(depth):
        i = layer // 2
        if layer % 2 == 0:
            x2d = _even_layer(x2d, mem2d, mem_norm_g, e_ln_g[i], e_w_in[i], e_w_mem_kv[i], e_w_out[i],
                              na_q_norm[i], na_k_norm[i], na_rpb[i], dn_conv_w[i], dn_a_log[i], dn_dt_bias[i],
                              dn_out_norm[i], e_mem_q_norm[i], e_mem_k_norm[i], batch=batch)
        else:
            x2d = _odd_layer(x2d, mem2d, mem_norm_g, layer, o_ln_g[i], o_w_in[i], o_w_mem_kv[i], o_w_out[i],
                             df_q_norm[i], df_k_norm[i], df_lambda[i], df_subln[i], o_mem_q_norm[i],
                             o_mem_k_norm[i], batch=batch)
    return x2d.reshape(batch, t, d)
```

```python
import functools
import math

import numpy as np
import jax
import jax.numpy as jnp
from jax import lax
from jax.experimental import pallas as pl
from jax.experimental.pallas import tpu as pltpu

F32 = jnp.float32
BF16 = jnp.bfloat16

HEAD_DIM = 128
GRID_W = 64
NA_HEADS = 6
NA_WIN_H = 8
NA_WIN_W = 16
DN_HEADS = 6
DN_CONV = 5
DN_CHUNK = 64
DN_UNROLL = 4
MEM_HEADS = 4
DIFF_HEADS = 6
DIFF_VDIM = 2 * HEAD_DIM
DIFF_KC = 512
DIFF_BOUND_LIMIT = 50.0
ROPE_THETA = 10000.0
EPS = 1e-6
LOG2E = math.log2(math.e)

NA_W = NA_HEADS * HEAD_DIM
DN_W = DN_HEADS * HEAD_DIM
MEM_W = MEM_HEADS * HEAD_DIM
DIFF_W = DIFF_HEADS * DIFF_VDIM

LANES = 128
VMEM_LIMIT = 56 * 1024 * 1024

NA_QROWS = 4
NA_KROWS = 12
NA_TQ = NA_QROWS * GRID_W
NA_TK = NA_KROWS * GRID_W


def _cparams(*sem):
    return pltpu.CompilerParams(dimension_semantics=sem, vmem_limit_bytes=VMEM_LIMIT)


def _dot(a, b):
    return jnp.dot(a, b, preferred_element_type=F32)


def _dot_nt(a, b):
    return lax.dot_general(a, b, (((1,), (1,)), ((), ())), preferred_element_type=F32)


def _silu(x):
    return x * (1.0 / (1.0 + jnp.exp(-x)))


def _rms(x, g):
    return x * lax.rsqrt(jnp.mean(x * x, axis=-1, keepdims=True) + EPS) * g


def _proj_kernel(x_ref, g_ref, w_ref, *rest, has_aux):
    if has_aux:
        waux_ref, o_ref, aux_ref, h_ref = rest
    else:
        o_ref, h_ref = rest

    @pl.when(pl.program_id(1) == 0)
    def _():
        h = _rms(x_ref[...], g_ref[...]).astype(BF16)
        h_ref[...] = h
        if has_aux:
            aux_ref[...] = _dot(h, waux_ref[...])

    o_ref[...] = _dot(h_ref[...], w_ref[...]).astype(o_ref.dtype)


def _proj(x2d, g, w, w_aux=None, *, tm, tn, name):
    n, d = x2d.shape
    nc = w.shape[1]
    has_aux = w_aux is not None
    in_specs = [
        pl.BlockSpec((tm, d), lambda i, j: (i, 0)),
        pl.BlockSpec((1, d), lambda i, j: (0, 0)),
        pl.BlockSpec((d, tn), lambda i, j: (0, j)),
    ]
    out_shape = [jax.ShapeDtypeStruct((n, nc), BF16)]
    out_specs = [pl.BlockSpec((tm, tn), lambda i, j: (i, j))]
    args = [x2d, g.reshape(1, d), w]
    if has_aux:
        in_specs.append(pl.BlockSpec((d, LANES), lambda i, j: (0, 0)))
        out_shape.append(jax.ShapeDtypeStruct((n, LANES), F32))
        out_specs.append(pl.BlockSpec((tm, LANES), lambda i, j: (i, 0)))
        args.append(w_aux)
    res = pl.pallas_call(
        functools.partial(_proj_kernel, has_aux=has_aux),
        grid=(n // tm, nc // tn),
        in_specs=in_specs,
        out_specs=out_specs,
        out_shape=out_shape,
        scratch_shapes=[pltpu.VMEM((tm, d), BF16)],
        compiler_params=_cparams("parallel", "arbitrary"),
        name=name,
    )(*args)
    return res if has_aux else res[0]


def _outproj_kernel(*refs, n_in):
    x_ref = refs[0]
    a_refs = refs[1:1 + n_in]
    w_refs = refs[1 + n_in:1 + 2 * n_in]
    o_ref = refs[1 + 2 * n_in]
    acc = x_ref[...]
    for a_ref, w_ref in zip(a_refs, w_refs):
        acc = acc + _dot(a_ref[...], w_ref[...])
    o_ref[...] = acc


def _outproj(x2d, acts, weights, *, tm, name):
    n, d = x2d.shape
    n_in = len(acts)
    in_specs = [pl.BlockSpec((tm, d), lambda i: (i, 0))]
    in_specs += [pl.BlockSpec((tm, a.shape[1]), lambda i: (i, 0)) for a in acts]
    in_specs += [pl.BlockSpec(w.shape, lambda i: (0, 0)) for w in weights]
    return pl.pallas_call(
        functools.partial(_outproj_kernel, n_in=n_in),
        grid=(n // tm,),
        in_specs=in_specs,
        out_specs=pl.BlockSpec((tm, d), lambda i: (i, 0)),
        out_shape=jax.ShapeDtypeStruct((n, d), F32),
        compiler_params=_cparams("parallel"),
        name=name,
    )(x2d, *acts, *weights)


def _mem_kernel(q_ref, gate_ref, k_ref, v_ref, qn_ref, kn_ref, o_ref):
    scale = HEAD_DIM ** -0.5
    for h in range(MEM_HEADS):
        cols = slice(h * HEAD_DIM, (h + 1) * HEAD_DIM)
        q = (_rms(q_ref[:, cols].astype(F32), qn_ref[...]) * scale).astype(BF16)
        k = _rms(k_ref[:, cols].astype(F32), kn_ref[...]).astype(BF16)
        s = _dot_nt(q, k)
        e = jnp.exp(s - jnp.max(s, axis=-1, keepdims=True))
        p = (e * (1.0 / jnp.sum(e, axis=-1, keepdims=True))).astype(BF16)
        o = _dot(p, v_ref[:, cols])
        o_ref[:, cols] = (o * _silu(gate_ref[:, cols].astype(F32))).astype(o_ref.dtype)


def _mem_attention(proj, q_blk, gate_blk, kv, q_norm, k_norm, *, batch, tq, name):
    n = proj.shape[0]
    t = n // batch
    m = kv.shape[0] // batch
    steps = t // tq
    return pl.pallas_call(
        _mem_kernel,
        grid=(batch, steps),
        in_specs=[
            pl.BlockSpec((tq, MEM_W), lambda b, i: (b * steps + i, q_blk)),
            pl.BlockSpec((tq, MEM_W), lambda b, i: (b * steps + i, gate_blk)),
            pl.BlockSpec((m, MEM_W), lambda b, i: (b, 0)),
            pl.BlockSpec((m, MEM_W), lambda b, i: (b, 1)),
            pl.BlockSpec((1, HEAD_DIM), lambda b, i: (0, 0)),
            pl.BlockSpec((1, HEAD_DIM), lambda b, i: (0, 0)),
        ],
        out_specs=pl.BlockSpec((tq, MEM_W), lambda b, i: (b * steps + i, 0)),
        out_shape=jax.ShapeDtypeStruct((n, MEM_W), BF16),
        compiler_params=_cparams("parallel", "parallel"),
        name=name,
    )(proj, proj, kv, kv, q_norm.reshape(1, HEAD_DIM), k_norm.reshape(1, HEAD_DIM))


def _na_bias_kernel(rpb_ref, o_ref, *, n_rows):
    h = pl.program_id(0)
    n_dc = 2 * NA_WIN_W - 1
    qc = lax.broadcasted_iota(jnp.int32, (GRID_W, GRID_W), 0)
    kc = lax.broadcasted_iota(jnp.int32, (GRID_W, GRID_W), 1)
    dcm = jnp.clip(kc - qc, -(NA_WIN_W - 1), NA_WIN_W - 1) + (NA_WIN_W - 1)
    c0 = jnp.clip(qc - NA_WIN_W // 2, 0, GRID_W - NA_WIN_W)
    col_ok = (kc >= c0) & (kc < c0 + NA_WIN_W)
    neg = jnp.full((GRID_W, GRID_W), -jnp.inf, F32)
    tiles = []
    for dr in range(2 * NA_WIN_H - 1):
        t = jnp.zeros((GRID_W, GRID_W), F32)
        for dc in range(n_dc):
            t = jnp.where(dcm == dc, rpb_ref[h, dr * n_dc + dc], t)
        tiles.append(jnp.where(col_ok, t, neg))
    n_blocks = n_rows // NA_QROWS
    for case, j in enumerate((0, 1, n_blocks - 1)):
        start = min(max(NA_QROWS * j - NA_QROWS, 0), n_rows - NA_KROWS)
        for a in range(NA_QROWS):
            qr = NA_QROWS * j + a
            r0 = min(max(qr - NA_WIN_H // 2, 0), n_rows - NA_WIN_H)
            for b in range(NA_KROWS):
                kr = start + b
                tile = tiles[kr - qr + NA_WIN_H - 1] if r0 <= kr < r0 + NA_WIN_H else neg
                o_ref[0, case, a * GRID_W:(a + 1) * GRID_W, b * GRID_W:(b + 1) * GRID_W] = tile


def _na_bias(rpb, n_rows):
    h = rpb.shape[0]
    return pl.pallas_call(
        functools.partial(_na_bias_kernel, n_rows=n_rows),
        grid=(h,),
        in_specs=[pl.BlockSpec(memory_space=pltpu.SMEM)],
        out_specs=pl.BlockSpec((1, 3, NA_TQ, NA_TK), lambda i: (i, 0, 0, 0)),
        out_shape=jax.ShapeDtypeStruct((h, 3, NA_TQ, NA_TK), F32),
        compiler_params=_cparams("parallel"),
        name="na_bias",
    )(rpb.reshape(h, -1))


def _na_kernel(q_ref, k_ref, v_ref, gate_ref, bias_ref, qn_ref, kn_ref, o_ref, *, n_blocks):
    j = pl.program_id(2)
    start = pl.multiple_of(jnp.clip(j - 1, 0, n_blocks - NA_KROWS // NA_QROWS) * NA_TQ, NA_TQ)
    q = (_rms(q_ref[...].astype(F32), qn_ref[...]) * HEAD_DIM ** -0.5).astype(BF16)
    k = _rms(k_ref[pl.ds(start, NA_TK), :].astype(F32), kn_ref[...]).astype(BF16)
    s = _dot_nt(q, k) + bias_ref[0, 0]
    e = jnp.exp(s - jnp.max(s, axis=-1, keepdims=True))
    p = (e * (1.0 / jnp.sum(e, axis=-1, keepdims=True))).astype(BF16)
    o = _dot(p, v_ref[pl.ds(start, NA_TK), :])
    o_ref[...] = (o * _silu(gate_ref[...].astype(F32))).astype(o_ref.dtype)


def _na_attention(proj, bias, q_norm, k_norm, *, batch):
    n = proj.shape[0]
    t = n // batch
    n_blocks = t // NA_TQ
    last = n_blocks - 1

    def case(j):
        return jnp.where(j == 0, 0, jnp.where(j == last, 2, 1))

    return pl.pallas_call(
        functools.partial(_na_kernel, n_blocks=n_blocks),
        grid=(batch, NA_HEADS, n_blocks),
        in_specs=[
            pl.BlockSpec((NA_TQ, HEAD_DIM), lambda b, h, j: (b * n_blocks + j, h)),
            pl.BlockSpec((t, HEAD_DIM), lambda b, h, j: (b, NA_HEADS + h)),
            pl.BlockSpec((t, HEAD_DIM), lambda b, h, j: (b, 2 * NA_HEADS + h)),
            pl.BlockSpec((NA_TQ, HEAD_DIM), lambda b, h, j: (b * n_blocks + j, 3 * NA_HEADS + h)),
            pl.BlockSpec((1, 1, NA_TQ, NA_TK), lambda b, h, j: (h, case(j), 0, 0)),
            pl.BlockSpec((1, HEAD_DIM), lambda b, h, j: (0, 0)),
            pl.BlockSpec((1, HEAD_DIM), lambda b, h, j: (0, 0)),
        ],
        out_specs=pl.BlockSpec((NA_TQ, HEAD_DIM), lambda b, h, j: (b * n_blocks + j, h)),
        out_shape=jax.ShapeDtypeStruct((n, NA_W), BF16),
        compiler_params=_cparams("parallel", "parallel", "arbitrary"),
        name="na_attention",
    )(proj, proj, proj, proj, bias, q_norm.reshape(1, HEAD_DIM), k_norm.reshape(1, HEAD_DIM))


def _split3(x):
    hi = x.astype(BF16)
    r = x - hi.astype(F32)
    mid = r.astype(BF16)
    lo = (r - mid.astype(F32)).astype(BF16)
    return hi, mid, lo


def _dn_gates_kernel(ba_ref, alog_ref, dtb_ref, col_ref, row_ref):
    tm = ba_ref.shape[0]
    ba = ba_ref[...]
    lane = lax.broadcasted_iota(jnp.int32, (1, LANES), 1)
    beta = 1.0 / (1.0 + jnp.exp(-ba))
    z = ba + dtb_ref[...]
    softplus = jnp.maximum(z, 0.0) + jnp.log(1.0 + jnp.exp(-jnp.abs(z)))
    g = -jnp.exp(alog_ref[...]) * softplus
    r = lax.broadcasted_iota(jnp.int32, (tm, tm), 0)
    c = lax.broadcasted_iota(jnp.int32, (tm, tm), 1)
    same = (r // DN_CHUNK) == (c // DN_CHUNK)
    m_f = jnp.where(same & (c <= r), 1.0, 0.0).astype(BF16)
    m_b = jnp.where(same & (c >= r), 1.0, 0.0).astype(BF16)
    parts = _split3(g)
    g_f = _dot(m_f, parts[0]) + _dot(m_f, parts[1]) + _dot(m_f, parts[2])
    g_b = _dot(m_b, parts[0]) + _dot(m_b, parts[1]) + _dot(m_b, parts[2])
    nh = DN_HEADS
    out = jnp.where(lane < 2 * nh, beta, jnp.where(lane < 3 * nh, g_f, jnp.where(lane < 4 * nh, g_b, 0.0)))
    col_ref[...] = out
    row_ref[0] = out.T


def _dn_gates(ba, a_log, dt_bias, *, batch, tm=512):
    n = ba.shape[0]
    t = n // batch
    steps = t // tm
    pad = jnp.zeros((2 * DN_HEADS,), F32)
    alog_row = jnp.concatenate([pad, a_log.reshape(-1).astype(F32), jnp.zeros((LANES - 4 * DN_HEADS,), F32)])
    dtb_row = jnp.concatenate([pad, dt_bias.reshape(-1).astype(F32), jnp.zeros((LANES - 4 * DN_HEADS,), F32)])
    return pl.pallas_call(
        _dn_gates_kernel,
        grid=(batch, steps),
        in_specs=[
            pl.BlockSpec((tm, LANES), lambda b, i: (b * steps + i, 0)),
            pl.BlockSpec((1, LANES), lambda b, i: (0, 0)),
            pl.BlockSpec((1, LANES), lambda b, i: (0, 0)),
        ],
        out_specs=[
            pl.BlockSpec((tm, LANES), lambda b, i: (b * steps + i, 0)),
            pl.BlockSpec((1, LANES, tm), lambda b, i: (b, 0, i)),
        ],
        out_shape=[
            jax.ShapeDtypeStruct((n, LANES), F32),
            jax.ShapeDtypeStruct((batch, LANES, t), F32),
        ],
        compiler_params=_cparams("parallel", "parallel"),
        name="dn_gates",
    )(ba, alog_row.reshape(1, LANES), dtb_row.reshape(1, LANES))


def _dn_kernel(q_ref, k_ref, v_ref, gate_ref, cwq_ref, cwk_ref, cwv_ref, gc_ref, grf_ref, grb_ref,
               on_ref, o_ref, qn_s, kn_s, vn_s, u_s, wq_s, l2_s, of_s, ob_s):
    t = q_ref.shape[0]
    c_sz = DN_CHUNK
    n_chunks = t // c_sz
    head = pl.program_id(1)

    row = lax.broadcasted_iota(jnp.int32, (t, 1), 0)

    def conv_silu(x_ref, cw_ref):
        x = x_ref[...].astype(F32)
        acc = x * cw_ref[DN_CONV // 2:DN_CONV // 2 + 1, :]
        for tap in range(DN_CONV):
            off = tap - DN_CONV // 2
            if off == 0:
                continue
            shifted = pltpu.roll(x, (-off) % t, 0)
            ok = (row + off >= 0) & (row + off < t)
            acc = acc + jnp.where(ok, shifted, 0.0) * cw_ref[tap:tap + 1, :]
        return _silu(acc)

    def l2n(x):
        return x * lax.rsqrt(jnp.sum(x * x, axis=-1, keepdims=True) + EPS)

    qn_s[...] = l2n(conv_silu(q_ref, cwq_ref)) * HEAD_DIM ** -0.5
    kn_s[...] = l2n(conv_silu(k_ref, cwk_ref))
    vn_s[...] = conv_silu(v_ref, cwv_ref)

    lane = lax.broadcasted_iota(jnp.int32, (1, LANES), 1)
    ii = lax.broadcasted_iota(jnp.int32, (c_sz, c_sz), 0)
    jj = lax.broadcasted_iota(jnp.int32, (c_sz, c_sz), 1)

    n_levels = int(math.log2(c_sz))

    def local_body(i, carry):
        chains = []
        for u in range(DN_UNROLL):
            c = i * DN_UNROLL + u
            rows = pl.ds(pl.multiple_of(c * c_sz, c_sz), c_sz)
            kc = kn_s[rows, :]
            qc = qn_s[rows, :]
            vc = vn_s[rows, :]
            gcb = gc_ref[rows, :]
            k16 = kc.astype(BF16)
            kk = _dot_nt(k16, k16)
            qk_raw = _dot_nt(qc.astype(BF16), k16)
            for d, gr_ref in enumerate((grf_ref, grb_ref)):
                beta = jnp.sum(jnp.where(lane == d * DN_HEADS + head, gcb, 0.0), axis=1, keepdims=True)
                g_col = jnp.sum(jnp.where(lane == (2 + d) * DN_HEADS + head, gcb, 0.0), axis=1, keepdims=True)
                g_row = gr_ref[0, 0, pl.ds(c, 1), :]
                incl = (ii >= jj) if d == 0 else (ii <= jj)
                strict = (ii > jj) if d == 0 else (ii < jj)
                g_edge = g_row[:, c_sz - 1:c_sz] if d == 0 else g_row[:, 0:1]
                decay = jnp.where(incl, jnp.exp(jnp.where(incl, g_col - g_row, 0.0)), 0.0)
                e_g = jnp.exp(g_col)
                wq_s[d, c, c_sz:2 * c_sz, :] = (qc * e_g).astype(BF16)
                l2_s[d, c, 0:c_sz, :] = (qk_raw * decay).astype(BF16)
                kd = kc * jnp.exp(g_edge - g_col)
                l2_s[d, c, c_sz:c_sz + HEAD_DIM, :] = kd.T.astype(BF16)
                bm = jnp.where(strict, -(beta * kk * decay), 0.0)
                sol = jnp.concatenate([vc * beta, kc * (beta * e_g)], axis=1)
                chains.append([d, c, rows, bm, sol])
        for lvl in range(n_levels):
            last = lvl == n_levels - 1
            outs = []
            for _, _, _, bm, sol in chains:
                rhs = sol if last else jnp.concatenate([sol, bm], axis=1)
                outs.append(_dot(bm.astype(BF16), rhs.astype(BF16)))
            for chain, out in zip(chains, outs):
                chain[4] = chain[4] + out[:, :2 * HEAD_DIM]
                if not last:
                    chain[3] = out[:, 2 * HEAD_DIM:]
        for d, c, rows, _, sol in chains:
            u_s[d, rows, :] = sol[:, :HEAD_DIM]
            wq_s[d, c, 0:c_sz, :] = sol[:, HEAD_DIM:].astype(BF16)
        return carry

    lax.fori_loop(0, n_chunks // DN_UNROLL, local_body, 0)

    def scan_body(i, states):
        new_states = []
        for d, (gr_ref, o_s) in enumerate(((grf_ref, of_s), (grb_ref, ob_s))):
            c = i if d == 0 else n_chunks - 1 - i
            rows = pl.ds(pl.multiple_of(c * c_sz, c_sz), c_sz)
            s = states[d]
            g_row = gr_ref[0, 0, pl.ds(c, 1), :]
            g_edge = g_row[:, c_sz - 1:c_sz] if d == 0 else g_row[:, 0:1]
            ws_qs = _dot(wq_s[d, c], s.astype(BF16))
            v_new = u_s[d, rows, :] - ws_qs[:c_sz]
            both = _dot(l2_s[d, c], v_new.astype(BF16))
            o_s[rows, :] = ws_qs[c_sz:] + both[:c_sz]
            new_states.append(s * jnp.exp(g_edge) + both[c_sz:])
        return tuple(new_states)

    zero = jnp.zeros((HEAD_DIM, HEAD_DIM), F32)
    lax.fori_loop(0, n_chunks, scan_body, (zero, zero))

    o = _rms(of_s[...] + ob_s[...], on_ref[...])
    o_ref[...] = (o * _silu(gate_ref[...].astype(F32))).astype(o_ref.dtype)


def _deltanet(proj, conv_w, gates_col, gates_row, out_norm, *, batch, col0):
    n = proj.shape[0]
    t = n // batch
    n_chunks = t // DN_CHUNK
    nh = DN_HEADS
    gr = gates_row.reshape(batch, LANES, n_chunks, DN_CHUNK)
    return pl.pallas_call(
        _dn_kernel,
        grid=(batch, nh),
        in_specs=[
            pl.BlockSpec((t, HEAD_DIM), lambda b, h: (b, col0 + h)),
            pl.BlockSpec((t, HEAD_DIM), lambda b, h: (b, col0 + nh + h)),
            pl.BlockSpec((t, HEAD_DIM), lambda b, h: (b, col0 + 2 * nh + h)),
            pl.BlockSpec((t, HEAD_DIM), lambda b, h: (b, col0 + 3 * nh + h)),
            pl.BlockSpec((DN_CONV, HEAD_DIM), lambda b, h: (0, h)),
            pl.BlockSpec((DN_CONV, HEAD_DIM), lambda b, h: (0, nh + h)),
            pl.BlockSpec((DN_CONV, HEAD_DIM), lambda b, h: (0, 2 * nh + h)),
            pl.BlockSpec((t, LANES), lambda b, h: (b, 0)),
            pl.BlockSpec((1, 1, n_chunks, DN_CHUNK), lambda b, h: (b, 2 * nh + h, 0, 0)),
            pl.BlockSpec((1, 1, n_chunks, DN_CHUNK), lambda b, h: (b, 3 * nh + h, 0, 0)),
            pl.BlockSpec((1, HEAD_DIM), lambda b, h: (0, 0)),
        ],
        out_specs=pl.BlockSpec((t, HEAD_DIM), lambda b, h: (b, h)),
        out_shape=jax.ShapeDtypeStruct((n, DN_W), BF16),
        scratch_shapes=[
            pltpu.VMEM((t, HEAD_DIM), F32),
            pltpu.VMEM((t, HEAD_DIM), F32),
            pltpu.VMEM((t, HEAD_DIM), F32),
            pltpu.VMEM((2, t, HEAD_DIM), F32),
            pltpu.VMEM((2, n_chunks, 2 * DN_CHUNK, HEAD_DIM), BF16),
            pltpu.VMEM((2, n_chunks, DN_CHUNK + HEAD_DIM, DN_CHUNK), BF16),
            pltpu.VMEM((t, HEAD_DIM), F32),
            pltpu.VMEM((t, HEAD_DIM), F32),
        ],
        compiler_params=_cparams("parallel", "parallel"),
        name="deltanet",
    )(proj, proj, proj, proj, conv_w, conv_w, conv_w, gates_col, gr, gr, out_norm.reshape(1, HEAD_DIM))


def _rope_tables(t):
    inv = ROPE_THETA ** (-np.arange(0, HEAD_DIM, 2, dtype=np.float64) / HEAD_DIM)
    ang = (np.arange(t, dtype=np.float32)[:, None] * inv.astype(np.float32)[None, :]).astype(np.float64)
    cos, sin = np.cos(ang), np.sin(ang)
    cos_full = np.concatenate([cos, cos], axis=1).astype(np.float32)
    sin_signed = np.concatenate([-sin, sin], axis=1).astype(np.float32)
    return jnp.asarray(cos_full), jnp.asarray(sin_signed)


def _rope(x, cos, sin_signed):
    return x * cos + pltpu.roll(x, HEAD_DIM // 2, 1) * sin_signed


def _diff_kernel(q_ref, k_ref, v_ref, gate_ref, cosq_ref, sinq_ref, cosk_ref, sink_ref, qn_ref, kn_ref,
                 lam_ref, sub_ref, o_ref, k_s, kmax_s, *, lam_init):
    d = HEAD_DIM
    t = k_ref.shape[0]
    n_kc = t // DIFF_KC

    @pl.when(pl.program_id(2) == 0)
    def _():
        for m in range(2):
            k = _rms(k_ref[:, m * d:(m + 1) * d].astype(F32), kn_ref[...])
            k = _rope(k, cosk_ref[...], sink_ref[...])
            k_s[m] = k.astype(BF16)
            kmax = jnp.sqrt(jnp.max(jnp.sum(k * k, axis=-1, keepdims=True), axis=0, keepdims=True))
            kmax_s[m] = jnp.broadcast_to(kmax, kmax_s.shape[1:])

    lv = lam_ref[...]
    lam = (jnp.exp(jnp.sum(lv[0:1] * lv[1:2], axis=-1, keepdims=True))
           - jnp.exp(jnp.sum(lv[2:3] * lv[3:4], axis=-1, keepdims=True)) + lam_init)

    def scores(q16, m, c):
        return _dot_nt(q16, k_s[m, c * DIFF_KC:(c + 1) * DIFF_KC, :])

    stages = []
    for m in range(2):
        q = _rms(q_ref[:, m * d:(m + 1) * d].astype(F32), qn_ref[...])
        q = _rope(q, cosq_ref[...], sinq_ref[...]) * (d ** -0.5 * LOG2E)
        q16 = q.astype(BF16)
        bound = jnp.sqrt(jnp.sum(q * q, axis=-1, keepdims=True)) * kmax_s[m, 0:1, 0:1]

        def exact_max(q16=q16, m=m):
            mx = jnp.max(scores(q16, m, 0), axis=-1, keepdims=True)
            for c in range(1, n_kc):
                mx = jnp.maximum(mx, jnp.max(scores(q16, m, c), axis=-1, keepdims=True))
            return mx

        shift = lax.cond(jnp.max(bound) > DIFF_BOUND_LIMIT, exact_max, lambda bound=bound: bound)
        stages += [(m, c, q16, shift) for c in range(n_kc)]

    accs = [None, None]
    lsum = [None, None]
    s_next = scores(stages[0][2], stages[0][0], stages[0][1])
    for i, (m, c, q16, shift) in enumerate(stages):
        s_cur = s_next
        if i + 1 < len(stages):
            s_next = scores(stages[i + 1][2], stages[i + 1][0], stages[i + 1][1])
        e = jnp.exp2(s_cur - shift)
        part = e[:, 0:LANES]
        for j in range(1, DIFF_KC // LANES):
            part = part + e[:, j * LANES:(j + 1) * LANES]
        pv = _dot(e.astype(BF16), v_ref[c * DIFF_KC:(c + 1) * DIFF_KC, :])
        accs[m] = pv if accs[m] is None else accs[m] + pv
        lsum[m] = part if lsum[m] is None else lsum[m] + part
    outs = [accs[m] * (1.0 / jnp.sum(lsum[m], axis=-1, keepdims=True)) for m in range(2)]
    o = outs[0] - lam * outs[1]
    o = _rms(o, sub_ref[...]) * (1.0 - lam_init)
    o_ref[...] = (o * _silu(gate_ref[...].astype(F32))).astype(o_ref.dtype)


def _diff_attention(proj, q_norm, k_norm, lam_vecs, subln, *, batch, layer_idx, tq):
    n = proj.shape[0]
    t = n // batch
    steps = t // tq
    nh = DIFF_HEADS
    w = DIFF_VDIM
    cos, sin = _rope_tables(t)
    lam_init = 0.8 - 0.6 * math.exp(-0.3 * layer_idx)
    const = lambda shape: pl.BlockSpec(shape, lambda b, h, i: (0, 0))
    return pl.pallas_call(
        functools.partial(_diff_kernel, lam_init=lam_init),
        grid=(batch, nh, steps),
        in_specs=[
            pl.BlockSpec((tq, w), lambda b, h, i: (b * steps + i, h)),
            pl.BlockSpec((t, w), lambda b, h, i: (b, nh + h)),
            pl.BlockSpec((t, w), lambda b, h, i: (b, 2 * nh + h)),
            pl.BlockSpec((tq, w), lambda b, h, i: (b * steps + i, 3 * nh + h)),
            pl.BlockSpec((tq, HEAD_DIM), lambda b, h, i: (i, 0)),
            pl.BlockSpec((tq, HEAD_DIM), lambda b, h, i: (i, 0)),
            const((t, HEAD_DIM)),
            const((t, HEAD_DIM)),
            const((1, HEAD_DIM)),
            const((1, HEAD_DIM)),
            const((4, HEAD_DIM)),
            const((1, w)),
        ],
        out_specs=pl.BlockSpec((tq, w), lambda b, h, i: (b * steps + i, h)),
        out_shape=jax.ShapeDtypeStruct((n, DIFF_W), BF16),
        scratch_shapes=[pltpu.VMEM((2, t, HEAD_DIM), BF16), pltpu.VMEM((2, 8, LANES), F32)],
        compiler_params=_cparams("parallel", "parallel", "arbitrary"),
        name="diff_attention",
    )(proj, proj, proj, proj, cos, sin, cos, sin, q_norm.reshape(1, HEAD_DIM), k_norm.reshape(1, HEAD_DIM),
      lam_vecs.astype(F32), subln.reshape(1, w))


def _even_layer(x2d, mem2d, mem_g, ln_g, w_in, w_mem_kv, w_out, na_q_norm, na_k_norm, na_rpb, dn_conv_w,
                dn_a_log, dn_dt_bias, dn_out_norm, mem_q_norm, mem_k_norm, *, batch):
    t = x2d.shape[0] // batch
    ba0 = 4 * NA_W + 4 * DN_W
    ba1 = ba0 + 4 * DN_HEADS
    w_main = jnp.concatenate([w_in[:, :ba0], w_in[:, ba1:]], axis=1).astype(BF16)
    w_ba = jnp.pad(w_in[:, ba0:ba1], ((0, 0), (0, LANES - 4 * DN_HEADS))).astype(BF16)
    proj, ba = _proj(x2d, ln_g, w_main, w_ba, tm=1024, tn=1024, name="even_in_proj")
    kv = _proj(mem2d, mem_g, w_mem_kv.astype(BF16), tm=mem2d.shape[0], tn=2 * MEM_W, name="even_mem_kv")

    bias = _na_bias(na_rpb.astype(F32), t // GRID_W)
    na_o = _na_attention(proj, bias, na_q_norm, na_k_norm, batch=batch)
    g_col, g_row = _dn_gates(ba, dn_a_log, dn_dt_bias, batch=batch)
    dn_o = _deltanet(proj, dn_conv_w.astype(F32), g_col, g_row, dn_out_norm, batch=batch,
                     col0=4 * NA_W // HEAD_DIM)
    mem_blk = ba0 // MEM_W
    mem_o = _mem_attention(proj, mem_blk, mem_blk + 1, kv, mem_q_norm, mem_k_norm, batch=batch, tq=1024,
                           name="even_mem_attention")
    w_o = w_out.astype(BF16)
    return _outproj(x2d, [na_o, dn_o, mem_o],
                    [w_o[:NA_W], w_o[NA_W:NA_W + DN_W], w_o[NA_W + DN_W:]], tm=512, name="even_out_proj")


def _odd_layer(x2d, mem2d, mem_g, layer_idx, ln_g, w_in, w_mem_kv, w_out, q_norm, k_norm, lam_vecs, subln_g,
               mem_q_norm, mem_k_norm, *, batch):
    proj = _proj(x2d, ln_g, w_in.astype(BF16), tm=1024, tn=1024, name="odd_in_proj")
    kv = _proj(mem2d, mem_g, w_mem_kv.astype(BF16), tm=mem2d.shape[0], tn=2 * MEM_W, name="odd_mem_kv")
    diff_o = _diff_attention(proj, q_norm, k_norm, lam_vecs, subln_g, batch=batch, layer_idx=layer_idx, tq=512)
    mem_blk = 4 * DIFF_W // MEM_W
    mem_o = _mem_attention(proj, mem_blk, mem_blk + 1, kv, mem_q_norm, mem_k_norm, batch=batch, tq=1024,
                           name="odd_mem_attention")
    w_o = w_out.astype(BF16)
    return _outproj(x2d, [diff_o, mem_o], [w_o[:DIFF_W], w_o[DIFF_W:]], tm=512, name="odd_out_proj")


def kernel(x, mem, mem_norm_g, e_ln_g, e_w_in, e_w_mem_kv, e_w_out, na_q_norm, na_k_norm, na_rpb, dn_conv_w,
           dn_a_log, dn_dt_bias, dn_out_norm, e_mem_q_norm, e_mem_k_norm, o_ln_g, o_w_in, o_w_mem_kv, o_w_out,
           df_q_norm, df_k_norm, df_lambda, df_subln, o_mem_q_norm, o_mem_k_norm):
    batch, t, d = x.shape
    depth = e_ln_g.shape[0] + o_ln_g.shape[0]
    x2d = x.reshape(batch * t, d)
    mem2d = mem.reshape(-1, d)
    for layer in range(depth):
        i = layer // 2
        if layer % 2 == 0:
            x2d = _even_layer(x2d, mem2d, mem_norm_g, e_ln_g[i], e_w_in[i], e_w_mem_kv[i], e_w_out[i],
                              na_q_norm[i], na_k_norm[i], na_rpb[i], dn_conv_w[i], dn_a_log[i], dn_dt_bias[i],
                              dn_out_norm[i], e_mem_q_norm[i], e_mem_k_norm[i], batch=batch)
        else:
            x2d = _odd_layer(x2d, mem2d, mem_norm_g, layer, o_ln_g[i], o_w_in[i], o_w_mem_kv[i], o_w_out[i],
                             df_q_norm[i], df_k_norm[i], df_lambda[i], df_subln[i], o_mem_q_norm[i],
                             o_mem_k_norm[i], batch=batch)
    return x2d.reshape(batch, t, d)
```

```python
import functools
import math

import numpy as np
import jax
import jax.numpy as jnp
from jax import lax
from jax.experimental import pallas as pl
from jax.experimental.pallas import tpu as pltpu

F32 = jnp.float32
BF16 = jnp.bfloat16

HEAD_DIM = 128
GRID_W = 64
NA_HEADS = 6
NA_WIN_H = 8
NA_WIN_W = 16
DN_HEADS = 6
DN_CONV = 5
DN_CHUNK = 64
DN_UNROLL = 4
MEM_HEADS = 4
DIFF_HEADS = 6
DIFF_VDIM = 2 * HEAD_DIM
DIFF_KC = 512
DIFF_BOUND_LIMIT = 50.0
ROPE_THETA = 10000.0
EPS = 1e-6
LOG2E = math.log2(math.e)

NA_W = NA_HEADS * HEAD_DIM
DN_W = DN_HEADS * HEAD_DIM
MEM_W = MEM_HEADS * HEAD_DIM
DIFF_W = DIFF_HEADS * DIFF_VDIM

LANES = 128
VMEM_LIMIT = 56 * 1024 * 1024

NA_QROWS = 4
NA_KROWS = 12
NA_TQ = NA_QROWS * GRID_W
NA_TK = NA_KROWS * GRID_W


def _cparams(*sem):
    return pltpu.CompilerParams(dimension_semantics=sem, vmem_limit_bytes=VMEM_LIMIT)


def _dot(a, b):
    return jnp.dot(a, b, preferred_element_type=F32)


def _dot_nt(a, b):
    return lax.dot_general(a, b, (((1,), (1,)), ((), ())), preferred_element_type=F32)


def _silu(x):
    return x * (0.5 * jnp.tanh(0.5 * x) + 0.5)


def _rms(x, g):
    return x * lax.rsqrt(jnp.mean(x * x, axis=-1, keepdims=True) + EPS) * g


def _proj_kernel(x_ref, g_ref, *rest, tiles, has_aux):
    w_refs = rest[:len(tiles)]
    rest = rest[len(tiles):]
    if has_aux:
        waux_ref, o_ref, aux_ref, h_ref = rest
    else:
        o_ref, h_ref = rest
    j = pl.program_id(1)

    @pl.when(j == 0)
    def _():
        h = _rms(x_ref[...], g_ref[...]).astype(BF16)
        h_ref[...] = h
        if has_aux:
            aux_ref[...] = _dot(h, waux_ref[...])

    first = 0
    for w_ref, n_tiles in zip(w_refs, tiles):
        @pl.when((j >= first) & (j < first + n_tiles))
        def _(w_ref=w_ref):
            o_ref[...] = _dot(h_ref[...], w_ref[...]).astype(o_ref.dtype)
        first += n_tiles


def _proj(x2d, g, w_groups, w_aux=None, *, tm, tn, name):
    n, d = x2d.shape
    tiles = tuple(nt for _, _, nt in w_groups)
    nc = sum(tiles) * tn
    has_aux = w_aux is not None
    in_specs = [
        pl.BlockSpec((tm, d), lambda i, j: (i, 0)),
        pl.BlockSpec((1, d), lambda i, j: (0, 0)),
    ]
    first = 0
    for _, tile0, nt in w_groups:
        in_specs.append(pl.BlockSpec(
            (d, tn), lambda i, j, first=first, tile0=tile0, nt=nt: (0, tile0 + jnp.clip(j - first, 0, nt - 1))))
        first += nt
    out_shape = [jax.ShapeDtypeStruct((n, nc), BF16)]
    out_specs = [pl.BlockSpec((tm, tn), lambda i, j: (i, j))]
    args = [x2d, g.reshape(1, d)] + [w for w, _, _ in w_groups]
    if has_aux:
        in_specs.append(pl.BlockSpec((d, LANES), lambda i, j: (0, 0)))
        out_shape.append(jax.ShapeDtypeStruct((n, LANES), F32))
        out_specs.append(pl.BlockSpec((tm, LANES), lambda i, j: (i, 0)))
        args.append(w_aux)
    res = pl.pallas_call(
        functools.partial(_proj_kernel, tiles=tiles, has_aux=has_aux),
        grid=(n // tm, nc // tn),
        in_specs=in_specs,
        out_specs=out_specs,
        out_shape=out_shape,
        scratch_shapes=[pltpu.VMEM((tm, d), BF16)],
        compiler_params=_cparams("parallel", "arbitrary"),
        name=name,
    )(*args)
    return res if has_aux else res[0]


def _outproj_kernel(*refs, n_in):
    x_ref = refs[0]
    a_refs = refs[1:1 + n_in]
    w_refs = refs[1 + n_in:1 + 2 * n_in]
    o_ref = refs[1 + 2 * n_in]
    acc = x_ref[...]
    for a_ref, w_ref in zip(a_refs, w_refs):
        acc = acc + _dot(a_ref[...], w_ref[...])
    o_ref[...] = acc


def _outproj(x2d, acts, w, *, tm, name):
    n, d = x2d.shape
    n_in = len(acts)
    in_specs = [pl.BlockSpec((tm, d), lambda i: (i, 0))]
    in_specs += [pl.BlockSpec((tm, a.shape[1]), lambda i: (i, 0)) for a in acts]
    row0 = 0
    for a in acts:
        rows = a.shape[1]
        assert row0 % rows == 0, "a weight row group must start at a multiple of its height"
        in_specs.append(pl.BlockSpec((rows, d), lambda i, blk=row0 // rows: (blk, 0)))
        row0 += rows
    weights = [w] * n_in
    return pl.pallas_call(
        functools.partial(_outproj_kernel, n_in=n_in),
        grid=(n // tm,),
        in_specs=in_specs,
        out_specs=pl.BlockSpec((tm, d), lambda i: (i, 0)),
        out_shape=jax.ShapeDtypeStruct((n, d), F32),
        compiler_params=_cparams("parallel"),
        name=name,
    )(x2d, *acts, *weights)


def _mem_kernel(q_ref, gate_ref, k_ref, v_ref, qn_ref, kn_ref, o_ref):
    scale = HEAD_DIM ** -0.5
    for h in range(MEM_HEADS):
        cols = slice(h * HEAD_DIM, (h + 1) * HEAD_DIM)
        q = (_rms(q_ref[:, cols].astype(F32), qn_ref[...]) * scale).astype(BF16)
        k = _rms(k_ref[:, cols].astype(F32), kn_ref[...]).astype(BF16)
        s = _dot_nt(q, k)
        e = jnp.exp(s - jnp.max(s, axis=-1, keepdims=True))
        p = (e * (1.0 / jnp.sum(e, axis=-1, keepdims=True))).astype(BF16)
        o = _dot(p, v_ref[:, cols])
        o_ref[:, cols] = (o * _silu(gate_ref[:, cols].astype(F32))).astype(o_ref.dtype)


def _mem_attention(proj, q_blk, gate_blk, kv, q_norm, k_norm, *, batch, tq, name):
    n = proj.shape[0]
    t = n // batch
    m = kv.shape[0] // batch
    steps = t // tq
    return pl.pallas_call(
        _mem_kernel,
        grid=(batch, steps),
        in_specs=[
            pl.BlockSpec((tq, MEM_W), lambda b, i: (b * steps + i, q_blk)),
            pl.BlockSpec((tq, MEM_W), lambda b, i: (b * steps + i, gate_blk)),
            pl.BlockSpec((m, MEM_W), lambda b, i: (b, 0)),
            pl.BlockSpec((m, MEM_W), lambda b, i: (b, 1)),
            pl.BlockSpec((1, HEAD_DIM), lambda b, i: (0, 0)),
            pl.BlockSpec((1, HEAD_DIM), lambda b, i: (0, 0)),
        ],
        out_specs=pl.BlockSpec((tq, MEM_W), lambda b, i: (b * steps + i, 0)),
        out_shape=jax.ShapeDtypeStruct((n, MEM_W), BF16),
        compiler_params=_cparams("parallel", "parallel"),
        name=name,
    )(proj, proj, kv, kv, q_norm.reshape(1, HEAD_DIM), k_norm.reshape(1, HEAD_DIM))


def _na_bias_kernel(rpb_ref, o_ref, *, n_rows):
    h = pl.program_id(0)
    n_dc = 2 * NA_WIN_W - 1
    qc = lax.broadcasted_iota(jnp.int32, (GRID_W, GRID_W), 0)
    kc = lax.broadcasted_iota(jnp.int32, (GRID_W, GRID_W), 1)
    dcm = jnp.clip(kc - qc, -(NA_WIN_W - 1), NA_WIN_W - 1) + (NA_WIN_W - 1)
    c0 = jnp.clip(qc - NA_WIN_W // 2, 0, GRID_W - NA_WIN_W)
    col_ok = (kc >= c0) & (kc < c0 + NA_WIN_W)
    neg = jnp.full((GRID_W, GRID_W), -jnp.inf, F32)
    tiles = []
    for dr in range(2 * NA_WIN_H - 1):
        t = jnp.zeros((GRID_W, GRID_W), F32)
        for dc in range(n_dc):
            t = jnp.where(dcm == dc, rpb_ref[h, dr * n_dc + dc], t)
        tiles.append(jnp.where(col_ok, t * LOG2E, neg))
    n_blocks = n_rows // NA_QROWS
    for case, j in enumerate((0, 1, n_blocks - 1)):
        start = min(max(NA_QROWS * j - NA_QROWS, 0), n_rows - NA_KROWS)
        for a in range(NA_QROWS):
            qr = NA_QROWS * j + a
            r0 = min(max(qr - NA_WIN_H // 2, 0), n_rows - NA_WIN_H)
            for b in range(NA_KROWS):
                kr = start + b
                tile = tiles[kr - qr + NA_WIN_H - 1] if r0 <= kr < r0 + NA_WIN_H else neg
                o_ref[0, case, a * GRID_W:(a + 1) * GRID_W, b * GRID_W:(b + 1) * GRID_W] = tile


def _na_bias(rpb, n_rows):
    h = rpb.shape[0]
    return pl.pallas_call(
        functools.partial(_na_bias_kernel, n_rows=n_rows),
        grid=(h,),
        in_specs=[pl.BlockSpec(memory_space=pltpu.SMEM)],
        out_specs=pl.BlockSpec((1, 3, NA_TQ, NA_TK), lambda i: (i, 0, 0, 0)),
        out_shape=jax.ShapeDtypeStruct((h, 3, NA_TQ, NA_TK), F32),
        compiler_params=_cparams("parallel"),
        name="na_bias",
    )(rpb.reshape(h, -1))


def _na_kernel(q_ref, k_ref, v_ref, gate_ref, bias_ref, qn_ref, kn_ref, o_ref, k_s, *, n_blocks):
    k_s[...] = _rms(k_ref[...].astype(F32), kn_ref[...]).astype(BF16)

    def key_start(j):
        return min(max(j - 1, 0), n_blocks - NA_KROWS // NA_QROWS) * NA_TQ

    def scores(j):
        q = _rms(q_ref[j * NA_TQ:(j + 1) * NA_TQ, :].astype(F32), qn_ref[...])
        q16 = (q * (HEAD_DIM ** -0.5 * LOG2E)).astype(BF16)
        case = 0 if j == 0 else (2 if j == n_blocks - 1 else 1)
        return _dot_nt(q16, k_s[key_start(j):key_start(j) + NA_TK, :]) + bias_ref[0, case]

    s_next = scores(0)
    for j in range(n_blocks):
        s = s_next
        if j + 1 < n_blocks:
            s_next = scores(j + 1)
        rows = slice(j * NA_TQ, (j + 1) * NA_TQ)
        e = jnp.exp2(s - jnp.max(s, axis=-1, keepdims=True))
        inv_l = 1.0 / jnp.sum(e, axis=-1, keepdims=True)
        o = _dot(e.astype(BF16), v_ref[key_start(j):key_start(j) + NA_TK, :]) * inv_l
        o_ref[rows, :] = (o * _silu(gate_ref[rows, :].astype(F32))).astype(o_ref.dtype)


def _na_attention(proj, bias, q_norm, k_norm, *, batch):
    n = proj.shape[0]
    t = n // batch
    n_blocks = t // NA_TQ
    nh = NA_HEADS
    head_block = lambda off: pl.BlockSpec((t, HEAD_DIM), lambda b, h: (b, off + h))
    return pl.pallas_call(
        functools.partial(_na_kernel, n_blocks=n_blocks),
        grid=(batch, nh),
        in_specs=[
            head_block(0),
            head_block(nh),
            head_block(2 * nh),
            head_block(3 * nh),
            pl.BlockSpec((1, 3, NA_TQ, NA_TK), lambda b, h: (h, 0, 0, 0)),
            pl.BlockSpec((1, HEAD_DIM), lambda b, h: (0, 0)),
            pl.BlockSpec((1, HEAD_DIM), lambda b, h: (0, 0)),
        ],
        out_specs=head_block(0),
        out_shape=jax.ShapeDtypeStruct((n, NA_W), BF16),
        scratch_shapes=[pltpu.VMEM((t, HEAD_DIM), BF16)],
        compiler_params=_cparams("parallel", "parallel"),
        name="na_attention",
    )(proj, proj, proj, proj, bias, q_norm.reshape(1, HEAD_DIM), k_norm.reshape(1, HEAD_DIM))


def _split3(x):
    hi = x.astype(BF16)
    r = x - hi.astype(F32)
    mid = r.astype(BF16)
    lo = (r - mid.astype(F32)).astype(BF16)
    return hi, mid, lo


def _dn_gates_kernel(ba_ref, alog_ref, dtb_ref, col_ref, row_ref):
    tm = ba_ref.shape[0]
    ba = ba_ref[...]
    lane = lax.broadcasted_iota(jnp.int32, (1, LANES), 1)
    beta = 1.0 / (1.0 + jnp.exp(-ba))
    z = ba + dtb_ref[...]
    softplus = jnp.maximum(z, 0.0) + jnp.log(1.0 + jnp.exp(-jnp.abs(z)))
    g = -jnp.exp(alog_ref[...]) * softplus
    r = lax.broadcasted_iota(jnp.int32, (tm, tm), 0)
    c = lax.broadcasted_iota(jnp.int32, (tm, tm), 1)
    same = (r // DN_CHUNK) == (c // DN_CHUNK)
    m_f = jnp.where(same & (c <= r), 1.0, 0.0).astype(BF16)
    m_b = jnp.where(same & (c >= r), 1.0, 0.0).astype(BF16)
    parts = _split3(g)
    g_f = _dot(m_f, parts[0]) + _dot(m_f, parts[1]) + _dot(m_f, parts[2])
    g_b = _dot(m_b, parts[0]) + _dot(m_b, parts[1]) + _dot(m_b, parts[2])
    nh = DN_HEADS
    out = jnp.where(lane < 2 * nh, beta, jnp.where(lane < 3 * nh, g_f, jnp.where(lane < 4 * nh, g_b, 0.0)))
    col_ref[...] = out
    row_ref[0] = out.T


def _dn_gates(ba, a_log, dt_bias, *, batch, tm=512):
    n = ba.shape[0]
    t = n // batch
    steps = t // tm
    pad = jnp.zeros((2 * DN_HEADS,), F32)
    alog_row = jnp.concatenate([pad, a_log.reshape(-1).astype(F32), jnp.zeros((LANES - 4 * DN_HEADS,), F32)])
    dtb_row = jnp.concatenate([pad, dt_bias.reshape(-1).astype(F32), jnp.zeros((LANES - 4 * DN_HEADS,), F32)])
    return pl.pallas_call(
        _dn_gates_kernel,
        grid=(batch, steps),
        in_specs=[
            pl.BlockSpec((tm, LANES), lambda b, i: (b * steps + i, 0)),
            pl.BlockSpec((1, LANES), lambda b, i: (0, 0)),
            pl.BlockSpec((1, LANES), lambda b, i: (0, 0)),
        ],
        out_specs=[
            pl.BlockSpec((tm, LANES), lambda b, i: (b * steps + i, 0)),
            pl.BlockSpec((1, LANES, tm), lambda b, i: (b, 0, i)),
        ],
        out_shape=[
            jax.ShapeDtypeStruct((n, LANES), F32),
            jax.ShapeDtypeStruct((batch, LANES, t), F32),
        ],
        compiler_params=_cparams("parallel", "parallel"),
        name="dn_gates",
    )(ba, alog_row.reshape(1, LANES), dtb_row.reshape(1, LANES))


def _dn_kernel(q_ref, k_ref, v_ref, gate_ref, cwq_ref, cwk_ref, cwv_ref, gc_ref, grf_ref, grb_ref,
               on_ref, o_ref, qn_s, kn_s, vn_s, u_s, wq_s, l2_s, of_s, ob_s):
    t = q_ref.shape[0]
    c_sz = DN_CHUNK
    n_chunks = t // c_sz
    head = pl.program_id(1)

    row = lax.broadcasted_iota(jnp.int32, (t, 1), 0)

    def conv_silu(x_ref, cw_ref):
        x = x_ref[...].astype(F32)
        acc = x * cw_ref[DN_CONV // 2:DN_CONV // 2 + 1, :]
        for tap in range(DN_CONV):
            off = tap - DN_CONV // 2
            if off == 0:
                continue
            shifted = pltpu.roll(x, (-off) % t, 0)
            ok = (row + off >= 0) & (row + off < t)
            acc = acc + jnp.where(ok, shifted, 0.0) * cw_ref[tap:tap + 1, :]
        return _silu(acc)

    def l2n(x):
        return x * lax.rsqrt(jnp.sum(x * x, axis=-1, keepdims=True) + EPS)

    qn_s[...] = l2n(conv_silu(q_ref, cwq_ref)) * HEAD_DIM ** -0.5
    kn_s[...] = l2n(conv_silu(k_ref, cwk_ref))
    vn_s[...] = conv_silu(v_ref, cwv_ref)

    lane = lax.broadcasted_iota(jnp.int32, (1, LANES), 1)
    ii = lax.broadcasted_iota(jnp.int32, (c_sz, c_sz), 0)
    jj = lax.broadcasted_iota(jnp.int32, (c_sz, c_sz), 1)

    n_levels = int(math.log2(c_sz))

    def local_body(i, carry):
        chains = []
        for u in range(DN_UNROLL):
            c = i * DN_UNROLL + u
            rows = pl.ds(pl.multiple_of(c * c_sz, c_sz), c_sz)
            kc = kn_s[rows, :]
            qc = qn_s[rows, :]
            vc = vn_s[rows, :]
            gcb = gc_ref[rows, :]
            k16 = kc.astype(BF16)
            kk = _dot_nt(k16, k16)
            qk_raw = _dot_nt(qc.astype(BF16), k16)
            for d, gr_ref in enumerate((grf_ref, grb_ref)):
                beta = jnp.sum(jnp.where(lane == d * DN_HEADS + head, gcb, 0.0), axis=1, keepdims=True)
                g_col = jnp.sum(jnp.where(lane == (2 + d) * DN_HEADS + head, gcb, 0.0), axis=1, keepdims=True)
                g_row = gr_ref[0, 0, pl.ds(c, 1), :]
                incl = (ii >= jj) if d == 0 else (ii <= jj)
                strict = (ii > jj) if d == 0 else (ii < jj)
                g_edge = g_row[:, c_sz - 1:c_sz] if d == 0 else g_row[:, 0:1]
                decay = jnp.where(incl, jnp.exp(jnp.where(incl, g_col - g_row, 0.0)), 0.0)
                e_g = jnp.exp(g_col)
                wq_s[d, c, c_sz:2 * c_sz, :] = (qc * e_g).astype(BF16)
                l2_s[d, c, 0:c_sz, :] = (qk_raw * decay).astype(BF16)
                kd = kc * jnp.exp(g_edge - g_col)
                l2_s[d, c, c_sz:c_sz + HEAD_DIM, :] = kd.T.astype(BF16)
                bm = jnp.where(strict, -(beta * kk * decay), 0.0)
                sol = jnp.concatenate([vc * beta, kc * (beta * e_g)], axis=1)
                chains.append([d, c, rows, bm, sol])
        for lvl in range(n_levels):
            last = lvl == n_levels - 1
            outs = []
            for _, _, _, bm, sol in chains:
                rhs = sol if last else jnp.concatenate([sol, bm], axis=1)
                outs.append(_dot(bm.astype(BF16), rhs.astype(BF16)))
            for chain, out in zip(chains, outs):
                chain[4] = chain[4] + out[:, :2 * HEAD_DIM]
                if not last:
                    chain[3] = out[:, 2 * HEAD_DIM:]
        for d, c, rows, _, sol in chains:
            u_s[d, rows, :] = sol[:, :HEAD_DIM]
            wq_s[d, c, 0:c_sz, :] = sol[:, HEAD_DIM:].astype(BF16)
        return carry

    lax.fori_loop(0, n_chunks // DN_UNROLL, local_body, 0)

    def scan_body(i, states):
        new_states = []
        for d, (gr_ref, o_s) in enumerate(((grf_ref, of_s), (grb_ref, ob_s))):
            c = i if d == 0 else n_chunks - 1 - i
            rows = pl.ds(pl.multiple_of(c * c_sz, c_sz), c_sz)
            s = states[d]
            g_row = gr_ref[0, 0, pl.ds(c, 1), :]
            g_edge = g_row[:, c_sz - 1:c_sz] if d == 0 else g_row[:, 0:1]
            ws_qs = _dot(wq_s[d, c], s.astype(BF16))
            v_new = u_s[d, rows, :] - ws_qs[:c_sz]
            both = _dot(l2_s[d, c], v_new.astype(BF16))
            o_s[rows, :] = ws_qs[c_sz:] + both[:c_sz]
            new_states.append(s * jnp.exp(g_edge) + both[c_sz:])
        return tuple(new_states)

    zero = jnp.zeros((HEAD_DIM, HEAD_DIM), F32)
    lax.fori_loop(0, n_chunks, scan_body, (zero, zero))

    o = _rms(of_s[...] + ob_s[...], on_ref[...])
    o_ref[...] = (o * _silu(gate_ref[...].astype(F32))).astype(o_ref.dtype)


def _deltanet(proj, conv_w, gates_col, gates_row, out_norm, *, batch, col0):
    n = proj.shape[0]
    t = n // batch
    n_chunks = t // DN_CHUNK
    nh = DN_HEADS
    gr = gates_row.reshape(batch, LANES, n_chunks, DN_CHUNK)
    return pl.pallas_call(
        _dn_kernel,
        grid=(batch, nh),
        in_specs=[
            pl.BlockSpec((t, HEAD_DIM), lambda b, h: (b, col0 + h)),
            pl.BlockSpec((t, HEAD_DIM), lambda b, h: (b, col0 + nh + h)),
            pl.BlockSpec((t, HEAD_DIM), lambda b, h: (b, col0 + 2 * nh + h)),
            pl.BlockSpec((t, HEAD_DIM), lambda b, h: (b, col0 + 3 * nh + h)),
            pl.BlockSpec((DN_CONV, HEAD_DIM), lambda b, h: (0, h)),
            pl.BlockSpec((DN_CONV, HEAD_DIM), lambda b, h: (0, nh + h)),
            pl.BlockSpec((DN_CONV, HEAD_DIM), lambda b, h: (0, 2 * nh + h)),
            pl.BlockSpec((t, LANES), lambda b, h: (b, 0)),
            pl.BlockSpec((1, 1, n_chunks, DN_CHUNK), lambda b, h: (b, 2 * nh + h, 0, 0)),
            pl.BlockSpec((1, 1, n_chunks, DN_CHUNK), lambda b, h: (b, 3 * nh + h, 0, 0)),
            pl.BlockSpec((1, HEAD_DIM), lambda b, h: (0, 0)),
        ],
        out_specs=pl.BlockSpec((t, HEAD_DIM), lambda b, h: (b, h)),
        out_shape=jax.ShapeDtypeStruct((n, DN_W), BF16),
        scratch_shapes=[
            pltpu.VMEM((t, HEAD_DIM), F32),
            pltpu.VMEM((t, HEAD_DIM), F32),
            pltpu.VMEM((t, HEAD_DIM), F32),
            pltpu.VMEM((2, t, HEAD_DIM), F32),
            pltpu.VMEM((2, n_chunks, 2 * DN_CHUNK, HEAD_DIM), BF16),
            pltpu.VMEM((2, n_chunks, DN_CHUNK + HEAD_DIM, DN_CHUNK), BF16),
            pltpu.VMEM((t, HEAD_DIM), F32),
            pltpu.VMEM((t, HEAD_DIM), F32),
        ],
        compiler_params=_cparams("parallel", "parallel"),
        name="deltanet",
    )(proj, proj, proj, proj, conv_w, conv_w, conv_w, gates_col, gr, gr, out_norm.reshape(1, HEAD_DIM))


def _rope_tables(t):
    inv = ROPE_THETA ** (-np.arange(0, HEAD_DIM, 2, dtype=np.float64) / HEAD_DIM)
    ang = (np.arange(t, dtype=np.float32)[:, None] * inv.astype(np.float32)[None, :]).astype(np.float64)
    cos, sin = np.cos(ang), np.sin(ang)
    cos_full = np.concatenate([cos, cos], axis=1).astype(np.float32)
    sin_signed = np.concatenate([-sin, sin], axis=1).astype(np.float32)
    return jnp.asarray(cos_full), jnp.asarray(sin_signed)


def _rope(x, cos, sin_signed):
    return x * cos + pltpu.roll(x, HEAD_DIM // 2, 1) * sin_signed


def _diff_kernel(q_ref, k_ref, v_ref, gate_ref, cosq_ref, sinq_ref, cosk_ref, sink_ref, qn_ref, kn_ref,
                 lam_ref, sub_ref, o_ref, k_s, e_s, *, lam_init):
    d = HEAD_DIM
    t = k_ref.shape[0]
    tq = q_ref.shape[0]
    n_kc = t // DIFF_KC

    @pl.when(pl.program_id(2) == 0)
    def _():
        for m in range(2):
            k = _rms(k_ref[:, m * d:(m + 1) * d].astype(F32), kn_ref[...])
            k_s[m] = _rope(k, cosk_ref[...], sink_ref[...]).astype(BF16)

    lv = lam_ref[...]
    lam = (jnp.exp(jnp.sum(lv[0:1] * lv[1:2], axis=-1, keepdims=True))
           - jnp.exp(jnp.sum(lv[2:3] * lv[3:4], axis=-1, keepdims=True)) + lam_init)

    def scores(q16, m, c):
        return _dot_nt(q16, k_s[m, c * DIFF_KC:(c + 1) * DIFF_KC, :])

    q16s = []
    for m in range(2):
        q = _rms(q_ref[:, m * d:(m + 1) * d].astype(F32), qn_ref[...])
        q = _rope(q, cosq_ref[...], sinq_ref[...]) * (d ** -0.5 * LOG2E)
        q16s.append(q.astype(BF16))

    g_max = lambda ref: jnp.max(jnp.abs(ref[...]), axis=-1, keepdims=True)
    bound = g_max(qn_ref) * g_max(kn_ref) * (d * d ** -0.5 * LOG2E)

    def exact_max():
        out = []
        for m in range(2):
            mx = jnp.max(scores(q16s[m], m, 0), axis=-1, keepdims=True)
            for c in range(1, n_kc):
                mx = jnp.maximum(mx, jnp.max(scores(q16s[m], m, c), axis=-1, keepdims=True))
            out.append(mx)
        return out

    shifts = lax.cond(bound[0, 0] > DIFF_BOUND_LIMIT, exact_max,
                      lambda: [jnp.broadcast_to(bound, (tq, 1))] * 2)

    def score_pair(c):
        return [scores(q16s[m], m, c) for m in range(2)]

    lsum = [None, None]
    s_next = score_pair(0)
    for c in range(n_kc):
        s_cur = s_next
        if c + 1 < n_kc:
            s_next = score_pair(c + 1)
        for m in range(2):
            e = jnp.exp2(s_cur[m] - shifts[m])
            part = e[:, 0:LANES]
            for j in range(1, DIFF_KC // LANES):
                part = part + e[:, j * LANES:(j + 1) * LANES]
            lsum[m] = part if lsum[m] is None else lsum[m] + part
            e_s[m, :, c * DIFF_KC:(c + 1) * DIFF_KC] = e.astype(BF16)
    l0 = jnp.sum(lsum[0], axis=-1, keepdims=True)
    l1 = jnp.sum(lsum[1], axis=-1, keepdims=True)
    r16 = (lam * l0 / l1).astype(BF16)
    half = tq // 2

    def mix_pair(c):
        cols = slice(c * DIFF_KC, (c + 1) * DIFF_KC)
        return [e_s[0, r * half:(r + 1) * half, cols] - r16[r * half:(r + 1) * half] *
                e_s[1, r * half:(r + 1) * half, cols] for r in range(2)]

    accs = [None, None]
    a_next = mix_pair(0)
    for c in range(n_kc):
        a_cur = a_next
        if c + 1 < n_kc:
            a_next = mix_pair(c + 1)
        for r in range(2):
            pv = _dot(a_cur[r], v_ref[c * DIFF_KC:(c + 1) * DIFF_KC, :])
            accs[r] = pv if accs[r] is None else accs[r] + pv
    o = jnp.concatenate(accs, axis=0) * (1.0 / l0)
    o = _rms(o, sub_ref[...]) * (1.0 - lam_init)
    o_ref[...] = (o * _silu(gate_ref[...].astype(F32))).astype(o_ref.dtype)


def _diff_attention(proj, q_norm, k_norm, lam_vecs, subln, *, batch, layer_idx, tq):
    n = proj.shape[0]
    t = n // batch
    steps = t // tq
    nh = DIFF_HEADS
    w = DIFF_VDIM
    cos, sin = _rope_tables(t)
    lam_init = 0.8 - 0.6 * math.exp(-0.3 * layer_idx)
    const = lambda shape: pl.BlockSpec(shape, lambda b, h, i: (0, 0))
    return pl.pallas_call(
        functools.partial(_diff_kernel, lam_init=lam_init),
        grid=(batch, nh, steps),
        in_specs=[
            pl.BlockSpec((tq, w), lambda b, h, i: (b * steps + i, h)),
            pl.BlockSpec((t, w), lambda b, h, i: (b, nh + h)),
            pl.BlockSpec((t, w), lambda b, h, i: (b, 2 * nh + h)),
            pl.BlockSpec((tq, w), lambda b, h, i: (b * steps + i, 3 * nh + h)),
            pl.BlockSpec((tq, HEAD_DIM), lambda b, h, i: (i, 0)),
            pl.BlockSpec((tq, HEAD_DIM), lambda b, h, i: (i, 0)),
            const((t, HEAD_DIM)),
            const((t, HEAD_DIM)),
            const((1, HEAD_DIM)),
            const((1, HEAD_DIM)),
            const((4, HEAD_DIM)),
            const((1, w)),
        ],
        out_specs=pl.BlockSpec((tq, w), lambda b, h, i: (b * steps + i, h)),
        out_shape=jax.ShapeDtypeStruct((n, DIFF_W), BF16),
        scratch_shapes=[
            pltpu.VMEM((2, t, HEAD_DIM), BF16),
            pltpu.VMEM((2, tq, t), BF16),
        ],
        compiler_params=_cparams("parallel", "parallel", "arbitrary"),
        name="diff_attention",
    )(proj, proj, proj, proj, cos, sin, cos, sin, q_norm.reshape(1, HEAD_DIM), k_norm.reshape(1, HEAD_DIM),
      lam_vecs.astype(F32), subln.reshape(1, w))


def _even_layer(x2d, mem2d, mem_g, ln_g, w_in, w_mem_kv, w_out, na_q_norm, na_k_norm, na_rpb, dn_conv_w,
                dn_a_log, dn_dt_bias, dn_out_norm, mem_q_norm, mem_k_norm, *, batch):
    t = x2d.shape[0] // batch
    ba0 = 4 * NA_W + 4 * DN_W
    ba1 = ba0 + 4 * DN_HEADS
    tn = 2 * MEM_W
    w16 = w_in.astype(BF16)
    w_mem = w16[:, ba1:]
    w_ba = jnp.pad(w16[:, ba0:ba1], ((0, 0), (0, LANES - 4 * DN_HEADS)))
    proj, ba = _proj(x2d, ln_g, [(w16, 0, ba0 // tn), (w_mem, 0, 1)], w_ba, tm=1024, tn=tn, name="even_in_proj")
    kv = _proj(mem2d, mem_g, [(w_mem_kv.astype(BF16), 0, 1)], tm=mem2d.shape[0], tn=tn, name="even_mem_kv")

    bias = _na_bias(na_rpb.astype(F32), t // GRID_W)
    na_o = _na_attention(proj, bias, na_q_norm, na_k_norm, batch=batch)
    g_col, g_row = _dn_gates(ba, dn_a_log, dn_dt_bias, batch=batch)
    dn_o = _deltanet(proj, dn_conv_w.astype(F32), g_col, g_row, dn_out_norm, batch=batch,
                     col0=4 * NA_W // HEAD_DIM)
    mem_blk = ba0 // MEM_W
    mem_o = _mem_attention(proj, mem_blk, mem_blk + 1, kv, mem_q_norm, mem_k_norm, batch=batch, tq=1024,
                           name="even_mem_attention")
    return _outproj(x2d, [na_o, dn_o, mem_o], w_out.astype(BF16), tm=512, name="even_out_proj")


def _odd_layer(x2d, mem2d, mem_g, layer_idx, ln_g, w_in, w_mem_kv, w_out, q_norm, k_norm, lam_vecs, subln_g,
               mem_q_norm, mem_k_norm, *, batch):
    tn = 2 * MEM_W
    proj = _proj(x2d, ln_g, [(w_in.astype(BF16), 0, w_in.shape[1] // tn)], tm=1024, tn=tn, name="odd_in_proj")
    kv = _proj(mem2d, mem_g, [(w_mem_kv.astype(BF16), 0, 1)], tm=mem2d.shape[0], tn=tn, name="odd_mem_kv")
    diff_o = _diff_attention(proj, q_norm, k_norm, lam_vecs, subln_g, batch=batch, layer_idx=layer_idx, tq=512)
    mem_blk = 4 * DIFF_W // MEM_W
    mem_o = _mem_attention(proj, mem_blk, mem_blk + 1, kv, mem_q_norm, mem_k_norm, batch=batch, tq=1024,
                           name="odd_mem_attention")
    return _outproj(x2d, [diff_o, mem_o], w_out.astype(BF16), tm=512, name="odd_out_proj")


def kernel(x, mem, mem_norm_g, e_ln_g, e_w_in, e_w_mem_kv, e_w_out, na_q_norm, na_k_norm, na_rpb, dn_conv_w,
           dn_a_log, dn_dt_bias, dn_out_norm, e_mem_q_norm, e_mem_k_norm, o_ln_g, o_w_in, o_w_mem_kv, o_w_out,
           df_q_norm, df_k_norm, df_lambda, df_subln, o_mem_q_norm, o_mem_k_norm):
    batch, t, d = x.shape
    depth = e_ln_g.shape[0] + o_ln_g.shape[0]
    x2d = x.reshape(batch * t, d)
    mem2d = mem.reshape(-1, d)
    for layer in range(depth):
        i = layer // 2
        if layer % 2 == 0:
            x2d = _even_layer(x2d, mem2d, mem_norm_g, e_ln_g[i], e_w_in[i], e_w_mem_kv[i], e_w_out[i],
                              na_q_norm[i], na_k_norm[i], na_rpb[i], dn_conv_w[i], dn_a_log[i], dn_dt_bias[i],
                              dn_out_norm[i], e_mem_q_norm[i], e_mem_k_norm[i], batch=batch)
        else:
            x2d = _odd_layer(x2d, mem2d, mem_norm_g, layer, o_ln_g[i], o_w_in[i], o_w_mem_kv[i], o_w_out[i],
                             df_q_norm[i], df_k_norm[i], df_lambda[i], df_subln[i], o_mem_q_norm[i],
                             o_mem_k_norm[i], batch=batch)
    return x2d.reshape(batch, t, d)
```

```python
import functools
import math

import numpy as np
import jax
import jax.numpy as jnp
from jax import lax
from jax.experimental import pallas as pl
from jax.experimental.pallas import tpu as pltpu

F32 = jnp.float32
BF16 = jnp.bfloat16

HEAD_DIM = 128
GRID_W = 64
NA_HEADS = 6
NA_WIN_H = 8
NA_WIN_W = 16
DN_HEADS = 6
DN_CONV = 5
DN_CHUNK = 64
DN_UNROLL = 8
MEM_HEADS = 4
DIFF_HEADS = 6
DIFF_VDIM = 2 * HEAD_DIM
DIFF_KC = 512
DIFF_BOUND_LIMIT = 50.0
ROPE_THETA = 10000.0
EPS = 1e-6
LOG2E = math.log2(math.e)

NA_W = NA_HEADS * HEAD_DIM
DN_W = DN_HEADS * HEAD_DIM
MEM_W = MEM_HEADS * HEAD_DIM
DIFF_W = DIFF_HEADS * DIFF_VDIM

LANES = 128
VMEM_LIMIT = 56 * 1024 * 1024

NA_QROWS = 4
NA_KROWS = 12
NA_TQ = NA_QROWS * GRID_W
NA_TK = NA_KROWS * GRID_W


def _cparams(*sem):
    return pltpu.CompilerParams(dimension_semantics=sem, vmem_limit_bytes=VMEM_LIMIT)


def _dot(a, b):
    return jnp.dot(a, b, preferred_element_type=F32)


def _dot_nt(a, b):
    return lax.dot_general(a, b, (((1,), (1,)), ((), ())), preferred_element_type=F32)


def _silu(x):
    return x * (0.5 * jnp.tanh(0.5 * x) + 0.5)


def _rowsum_mxu(x):
    ones = jnp.ones((LANES, LANES), BF16)
    hi = x.astype(BF16)
    lo = (x - hi.astype(F32)).astype(BF16)
    return _dot(hi, ones) + _dot(lo, ones)


def _rms(x, g):
    return x * lax.rsqrt(jnp.mean(x * x, axis=-1, keepdims=True) + EPS) * g


def _proj_kernel(x_ref, g_ref, *rest, tiles, has_aux):
    w_refs = rest[:len(tiles)]
    rest = rest[len(tiles):]
    if has_aux:
        waux_ref, o_ref, aux_ref, h_ref = rest
    else:
        o_ref, h_ref = rest
    j = pl.program_id(1)

    @pl.when(j == 0)
    def _():
        h = _rms(x_ref[...], g_ref[...]).astype(BF16)
        h_ref[...] = h
        if has_aux:
            aux_ref[...] = _dot(h, waux_ref[...])

    first = 0
    for w_ref, n_tiles in zip(w_refs, tiles):
        @pl.when((j >= first) & (j < first + n_tiles))
        def _(w_ref=w_ref):
            o_ref[...] = _dot(h_ref[...], w_ref[...]).astype(o_ref.dtype)
        first += n_tiles


def _proj(x2d, g, w_groups, w_aux=None, *, tm, tn, name):
    n, d = x2d.shape
    tiles = tuple(nt for _, _, nt in w_groups)
    nc = sum(tiles) * tn
    has_aux = w_aux is not None
    in_specs = [
        pl.BlockSpec((tm, d), lambda i, j: (i, 0)),
        pl.BlockSpec((1, d), lambda i, j: (0, 0)),
    ]
    first = 0
    for _, tile0, nt in w_groups:
        in_specs.append(pl.BlockSpec(
            (d, tn), lambda i, j, first=first, tile0=tile0, nt=nt: (0, tile0 + jnp.clip(j - first, 0, nt - 1))))
        first += nt
    out_shape = [jax.ShapeDtypeStruct((n, nc), BF16)]
    out_specs = [pl.BlockSpec((tm, tn), lambda i, j: (i, j))]
    args = [x2d, g.reshape(1, d)] + [w for w, _, _ in w_groups]
    if has_aux:
        in_specs.append(pl.BlockSpec((d, LANES), lambda i, j: (0, 0)))
        out_shape.append(jax.ShapeDtypeStruct((n, LANES), F32))
        out_specs.append(pl.BlockSpec((tm, LANES), lambda i, j: (i, 0)))
        args.append(w_aux)
    res = pl.pallas_call(
        functools.partial(_proj_kernel, tiles=tiles, has_aux=has_aux),
        grid=(n // tm, nc // tn),
        in_specs=in_specs,
        out_specs=out_specs,
        out_shape=out_shape,
        scratch_shapes=[pltpu.VMEM((tm, d), BF16)],
        compiler_params=_cparams("parallel", "arbitrary"),
        name=name,
    )(*args)
    return res if has_aux else res[0]


def _outproj_kernel(*refs, n_in):
    x_ref = refs[0]
    a_refs = refs[1:1 + n_in]
    w_refs = refs[1 + n_in:1 + 2 * n_in]
    o_ref = refs[1 + 2 * n_in]
    acc = x_ref[...]
    for a_ref, w_ref in zip(a_refs, w_refs):
        acc = acc + _dot(a_ref[...], w_ref[...])
    o_ref[...] = acc


def _outproj(x2d, acts, w, *, tm, name):
    n, d = x2d.shape
    n_in = len(acts)
    in_specs = [pl.BlockSpec((tm, d), lambda i: (i, 0))]
    in_specs += [pl.BlockSpec((tm, a.shape[1]), lambda i: (i, 0)) for a in acts]
    row0 = 0
    for a in acts:
        rows = a.shape[1]
        assert row0 % rows == 0, "a weight row group must start at a multiple of its height"
        in_specs.append(pl.BlockSpec((rows, d), lambda i, blk=row0 // rows: (blk, 0)))
        row0 += rows
    weights = [w] * n_in
    return pl.pallas_call(
        functools.partial(_outproj_kernel, n_in=n_in),
        grid=(n // tm,),
        in_specs=in_specs,
        out_specs=pl.BlockSpec((tm, d), lambda i: (i, 0)),
        out_shape=jax.ShapeDtypeStruct((n, d), F32),
        compiler_params=_cparams("parallel"),
        name=name,
    )(x2d, *acts, *weights)


def _mem_kernel(q_ref, gate_ref, k_ref, v_ref, qn_ref, kn_ref, o_ref):
    scale = HEAD_DIM ** -0.5
    for h in range(MEM_HEADS):
        cols = slice(h * HEAD_DIM, (h + 1) * HEAD_DIM)
        q = (_rms(q_ref[:, cols].astype(F32), qn_ref[...]) * scale).astype(BF16)
        k = _rms(k_ref[:, cols].astype(F32), kn_ref[...]).astype(BF16)
        s = _dot_nt(q, k)
        e = jnp.exp(s - jnp.max(s, axis=-1, keepdims=True))
        p = (e * (1.0 / jnp.sum(e, axis=-1, keepdims=True))).astype(BF16)
        o = _dot(p, v_ref[:, cols])
        o_ref[:, cols] = (o * _silu(gate_ref[:, cols].astype(F32))).astype(o_ref.dtype)


def _mem_attention(proj, q_blk, gate_blk, kv, q_norm, k_norm, *, batch, tq, name):
    n = proj.shape[0]
    t = n // batch
    m = kv.shape[0] // batch
    steps = t // tq
    return pl.pallas_call(
        _mem_kernel,
        grid=(batch, steps),
        in_specs=[
            pl.BlockSpec((tq, MEM_W), lambda b, i: (b * steps + i, q_blk)),
            pl.BlockSpec((tq, MEM_W), lambda b, i: (b * steps + i, gate_blk)),
            pl.BlockSpec((m, MEM_W), lambda b, i: (b, 0)),
            pl.BlockSpec((m, MEM_W), lambda b, i: (b, 1)),
            pl.BlockSpec((1, HEAD_DIM), lambda b, i: (0, 0)),
            pl.BlockSpec((1, HEAD_DIM), lambda b, i: (0, 0)),
        ],
        out_specs=pl.BlockSpec((tq, MEM_W), lambda b, i: (b * steps + i, 0)),
        out_shape=jax.ShapeDtypeStruct((n, MEM_W), BF16),
        compiler_params=_cparams("parallel", "parallel"),
        name=name,
    )(proj, proj, kv, kv, q_norm.reshape(1, HEAD_DIM), k_norm.reshape(1, HEAD_DIM))


def _na_bias_kernel(rpb_ref, o_ref, *, n_rows):
    h = pl.program_id(0)
    n_dc = 2 * NA_WIN_W - 1
    qc = lax.broadcasted_iota(jnp.int32, (GRID_W, GRID_W), 0)
    kc = lax.broadcasted_iota(jnp.int32, (GRID_W, GRID_W), 1)
    dcm = jnp.clip(kc - qc, -(NA_WIN_W - 1), NA_WIN_W - 1) + (NA_WIN_W - 1)
    c0 = jnp.clip(qc - NA_WIN_W // 2, 0, GRID_W - NA_WIN_W)
    col_ok = (kc >= c0) & (kc < c0 + NA_WIN_W)
    neg = jnp.full((GRID_W, GRID_W), -jnp.inf, F32)
    tiles = []
    for dr in range(2 * NA_WIN_H - 1):
        t = jnp.zeros((GRID_W, GRID_W), F32)
        for dc in range(n_dc):
            t = jnp.where(dcm == dc, rpb_ref[h, dr * n_dc + dc], t)
        tiles.append(jnp.where(col_ok, t * LOG2E, neg))
    n_blocks = n_rows // NA_QROWS
    for case, j in enumerate((0, 1, n_blocks - 1)):
        start = min(max(NA_QROWS * j - NA_QROWS, 0), n_rows - NA_KROWS)
        for a in range(NA_QROWS):
            qr = NA_QROWS * j + a
            r0 = min(max(qr - NA_WIN_H // 2, 0), n_rows - NA_WIN_H)
            for b in range(NA_KROWS):
                kr = start + b
                tile = tiles[kr - qr + NA_WIN_H - 1] if r0 <= kr < r0 + NA_WIN_H else neg
                o_ref[0, case, a * GRID_W:(a + 1) * GRID_W, b * GRID_W:(b + 1) * GRID_W] = tile


def _na_bias(rpb, n_rows):
    h = rpb.shape[0]
    return pl.pallas_call(
        functools.partial(_na_bias_kernel, n_rows=n_rows),
        grid=(h,),
        in_specs=[pl.BlockSpec(memory_space=pltpu.SMEM)],
        out_specs=pl.BlockSpec((1, 3, NA_TQ, NA_TK), lambda i: (i, 0, 0, 0)),
        out_shape=jax.ShapeDtypeStruct((h, 3, NA_TQ, NA_TK), F32),
        compiler_params=_cparams("parallel"),
        name="na_bias",
    )(rpb.reshape(h, -1))


def _na_kernel(q_ref, k_ref, v_ref, gate_ref, bias_ref, qn_ref, kn_ref, o_ref, k_s, *, n_blocks):
    k_s[...] = _rms(k_ref[...].astype(F32), kn_ref[...]).astype(BF16)

    def key_start(j):
        return min(max(j - 1, 0), n_blocks - NA_KROWS // NA_QROWS) * NA_TQ

    def scores(j):
        q = _rms(q_ref[j * NA_TQ:(j + 1) * NA_TQ, :].astype(F32), qn_ref[...])
        q16 = (q * (HEAD_DIM ** -0.5 * LOG2E)).astype(BF16)
        case = 0 if j == 0 else (2 if j == n_blocks - 1 else 1)
        return _dot_nt(q16, k_s[key_start(j):key_start(j) + NA_TK, :]) + bias_ref[0, case]

    s_next = scores(0)
    for j in range(n_blocks):
        s = s_next
        if j + 1 < n_blocks:
            s_next = scores(j + 1)
        rows = slice(j * NA_TQ, (j + 1) * NA_TQ)
        e = jnp.exp2(s - jnp.max(s, axis=-1, keepdims=True))
        inv_l = 1.0 / jnp.sum(e, axis=-1, keepdims=True)
        o = _dot(e.astype(BF16), v_ref[key_start(j):key_start(j) + NA_TK, :]) * inv_l
        o_ref[rows, :] = (o * _silu(gate_ref[rows, :].astype(F32))).astype(o_ref.dtype)


def _na_attention(proj, bias, q_norm, k_norm, *, batch):
    n = proj.shape[0]
    t = n // batch
    n_blocks = t // NA_TQ
    nh = NA_HEADS
    head_block = lambda off: pl.BlockSpec((t, HEAD_DIM), lambda b, h: (b, off + h))
    return pl.pallas_call(
        functools.partial(_na_kernel, n_blocks=n_blocks),
        grid=(batch, nh),
        in_specs=[
            head_block(0),
            head_block(nh),
            head_block(2 * nh),
            head_block(3 * nh),
            pl.BlockSpec((1, 3, NA_TQ, NA_TK), lambda b, h: (h, 0, 0, 0)),
            pl.BlockSpec((1, HEAD_DIM), lambda b, h: (0, 0)),
            pl.BlockSpec((1, HEAD_DIM), lambda b, h: (0, 0)),
        ],
        out_specs=head_block(0),
        out_shape=jax.ShapeDtypeStruct((n, NA_W), BF16),
        scratch_shapes=[pltpu.VMEM((t, HEAD_DIM), BF16)],
        compiler_params=_cparams("parallel", "parallel"),
        name="na_attention",
    )(proj, proj, proj, proj, bias, q_norm.reshape(1, HEAD_DIM), k_norm.reshape(1, HEAD_DIM))


def _split3(x):
    hi = x.astype(BF16)
    r = x - hi.astype(F32)
    mid = r.astype(BF16)
    lo = (r - mid.astype(F32)).astype(BF16)
    return hi, mid, lo


def _dn_gates_kernel(ba_ref, alog_ref, dtb_ref, col_ref, row_ref):
    tm = ba_ref.shape[0]
    ba = ba_ref[...]
    lane = lax.broadcasted_iota(jnp.int32, (1, LANES), 1)
    beta = 1.0 / (1.0 + jnp.exp(-ba))
    z = ba + dtb_ref[...]
    softplus = jnp.maximum(z, 0.0) + jnp.log(1.0 + jnp.exp(-jnp.abs(z)))
    g = -jnp.exp(alog_ref[...]) * softplus
    r = lax.broadcasted_iota(jnp.int32, (tm, tm), 0)
    c = lax.broadcasted_iota(jnp.int32, (tm, tm), 1)
    same = (r // DN_CHUNK) == (c // DN_CHUNK)
    m_f = jnp.where(same & (c <= r), 1.0, 0.0).astype(BF16)
    m_b = jnp.where(same & (c >= r), 1.0, 0.0).astype(BF16)
    parts = _split3(g)
    g_f = _dot(m_f, parts[0]) + _dot(m_f, parts[1]) + _dot(m_f, parts[2])
    g_b = _dot(m_b, parts[0]) + _dot(m_b, parts[1]) + _dot(m_b, parts[2])
    nh = DN_HEADS
    out = jnp.where(lane < 2 * nh, beta, jnp.where(lane < 3 * nh, g_f, jnp.where(lane < 4 * nh, g_b, 0.0)))
    col_ref[...] = out
    row_ref[0] = out.T


def _dn_gates(ba, a_log, dt_bias, *, batch, tm=512):
    n = ba.shape[0]
    t = n // batch
    steps = t // tm
    pad = jnp.zeros((2 * DN_HEADS,), F32)
    alog_row = jnp.concatenate([pad, a_log.reshape(-1).astype(F32), jnp.zeros((LANES - 4 * DN_HEADS,), F32)])
    dtb_row = jnp.concatenate([pad, dt_bias.reshape(-1).astype(F32), jnp.zeros((LANES - 4 * DN_HEADS,), F32)])
    return pl.pallas_call(
        _dn_gates_kernel,
        grid=(batch, steps),
        in_specs=[
            pl.BlockSpec((tm, LANES), lambda b, i: (b * steps + i, 0)),
            pl.BlockSpec((1, LANES), lambda b, i: (0, 0)),
            pl.BlockSpec((1, LANES), lambda b, i: (0, 0)),
        ],
        out_specs=[
            pl.BlockSpec((tm, LANES), lambda b, i: (b * steps + i, 0)),
            pl.BlockSpec((1, LANES, tm), lambda b, i: (b, 0, i)),
        ],
        out_shape=[
            jax.ShapeDtypeStruct((n, LANES), F32),
            jax.ShapeDtypeStruct((batch, LANES, t), F32),
        ],
        compiler_params=_cparams("parallel", "parallel"),
        name="dn_gates",
    )(ba, alog_row.reshape(1, LANES), dtb_row.reshape(1, LANES))


def _dn_kernel(q_ref, k_ref, v_ref, gate_ref, cwq_ref, cwk_ref, cwv_ref, gc_ref, grf_ref, grb_ref,
               on_ref, o_ref, xp_s, qn_s, kn_s, vn_s, mq_s, n_s, of_s, ob_s):
    t = q_ref.shape[0]
    c_sz = DN_CHUNK
    n_chunks = t // c_sz
    head = pl.program_id(1)

    border = jnp.zeros((8, HEAD_DIM), F32)
    xp_s[0:8, :] = border
    xp_s[t + 8:t + 16, :] = border

    def conv_silu(x_ref, cw_ref):
        xp_s[8:t + 8, :] = x_ref[...].astype(F32)
        acc = None
        for tap in range(DN_CONV):
            start = 8 + tap - DN_CONV // 2
            term = xp_s[start:start + t, :] * cw_ref[tap:tap + 1, :]
            acc = term if acc is None else acc + term
        return _silu(acc)

    def l2n(x):
        return x * lax.rsqrt(_rowsum_mxu(x * x) + EPS)

    qn_s[...] = l2n(conv_silu(q_ref, cwq_ref)) * HEAD_DIM ** -0.5
    kn_s[...] = l2n(conv_silu(k_ref, cwk_ref))
    vn_s[...] = conv_silu(v_ref, cwv_ref)

    lane = lax.broadcasted_iota(jnp.int32, (1, LANES), 1)
    ii = lax.broadcasted_iota(jnp.int32, (c_sz, c_sz), 0)
    jj = lax.broadcasted_iota(jnp.int32, (c_sz, c_sz), 1)

    n_levels = int(math.log2(c_sz))

    def local_body(i, carry):
        chains = []
        for u in range(DN_UNROLL):
            c = i * DN_UNROLL + u
            rows = pl.ds(pl.multiple_of(c * c_sz, c_sz), c_sz)
            kc = kn_s[rows, :]
            qc = qn_s[rows, :]
            vc = vn_s[rows, :]
            gcb = gc_ref[rows, :]
            k16 = kc.astype(BF16)
            kk = _dot_nt(k16, k16)
            qk_raw = _dot_nt(qc.astype(BF16), k16)
            for d, gr_ref in enumerate((grf_ref, grb_ref)):
                beta = jnp.sum(jnp.where(lane == d * DN_HEADS + head, gcb, 0.0), axis=1, keepdims=True)
                g_col = jnp.sum(jnp.where(lane == (2 + d) * DN_HEADS + head, gcb, 0.0), axis=1, keepdims=True)
                g_row = gr_ref[0, 0, pl.ds(c, 1), :]
                incl = (ii >= jj) if d == 0 else (ii <= jj)
                strict = (ii > jj) if d == 0 else (ii < jj)
                g_edge = g_row[:, c_sz - 1:c_sz] if d == 0 else g_row[:, 0:1]
                decay = jnp.where(incl, jnp.exp(jnp.where(incl, g_col - g_row, 0.0)), 0.0)
                e_g = jnp.exp(g_col)
                q_dec = qc * e_g
                kd = kc * jnp.exp(g_edge - g_col)
                lhs = jnp.concatenate([kd.T.astype(BF16), (qk_raw * decay).astype(BF16)], axis=0)
                bm = jnp.where(strict, -(beta * kk * decay), 0.0)
                sol = jnp.concatenate([vc * beta, kc * (beta * e_g)], axis=1)
                chains.append([d, c, rows, bm, sol, lhs, q_dec])
        for lvl in range(n_levels):
            last = lvl == n_levels - 1
            outs = []
            for _, _, _, bm, sol, _, _ in chains:
                rhs = sol if last else jnp.concatenate([sol, bm], axis=1)
                outs.append(_dot(bm.astype(BF16), rhs.astype(BF16)))
            for chain, out in zip(chains, outs):
                chain[4] = chain[4] + out[:, :2 * HEAD_DIM]
                if not last:
                    chain[3] = out[:, 2 * HEAD_DIM:]
        prods = [_dot(lhs, sol.astype(BF16)) for _, _, _, _, sol, lhs, _ in chains]
        for (d, c, rows, _, _, _, q_dec), x in zip(chains, prods):
            n_s[d, c] = x[:HEAD_DIM, :HEAD_DIM]
            mq_s[d, c, 0:HEAD_DIM, :] = x[:HEAD_DIM, HEAD_DIM:].astype(BF16)
            mq_s[d, c, HEAD_DIM:HEAD_DIM + c_sz, :] = (q_dec - x[HEAD_DIM:, HEAD_DIM:]).astype(BF16)
            (of_s, ob_s)[d][rows, :] = x[HEAD_DIM:, :HEAD_DIM]
        return carry

    lax.fori_loop(0, n_chunks // DN_UNROLL, local_body, 0)

    def scan_body(i, states):
        cs = (i, n_chunks - 1 - i)
        ys = [_dot(mq_s[d, cs[d]], states[d].astype(BF16)) for d in range(2)]
        new_states = []
        for d, (gr_ref, o_s) in enumerate(((grf_ref, of_s), (grb_ref, ob_s))):
            rows = pl.ds(pl.multiple_of(cs[d] * c_sz, c_sz), c_sz)
            g_row = gr_ref[0, 0, pl.ds(cs[d], 1), :]
            g_edge = g_row[:, c_sz - 1:c_sz] if d == 0 else g_row[:, 0:1]
            o_s[rows, :] = o_s[rows, :] + ys[d][HEAD_DIM:]
            new_states.append(states[d] * jnp.exp(g_edge) + n_s[d, cs[d]] - ys[d][:HEAD_DIM])
        return tuple(new_states)

    zero = jnp.zeros((HEAD_DIM, HEAD_DIM), F32)
    lax.fori_loop(0, n_chunks, scan_body, (zero, zero))

    o = of_s[...] + ob_s[...]
    o = o * lax.rsqrt(_rowsum_mxu(o * o) * (1.0 / HEAD_DIM) + EPS) * on_ref[...]
    o_ref[...] = (o * _silu(gate_ref[...].astype(F32))).astype(o_ref.dtype)


def _deltanet(proj, conv_w, gates_col, gates_row, out_norm, *, batch, col0):
    n = proj.shape[0]
    t = n // batch
    n_chunks = t // DN_CHUNK
    nh = DN_HEADS
    gr = gates_row.reshape(batch, LANES, n_chunks, DN_CHUNK)
    return pl.pallas_call(
        _dn_kernel,
        grid=(batch, nh),
        in_specs=[
            pl.BlockSpec((t, HEAD_DIM), lambda b, h: (b, col0 + h)),
            pl.BlockSpec((t, HEAD_DIM), lambda b, h: (b, col0 + nh + h)),
            pl.BlockSpec((t, HEAD_DIM), lambda b, h: (b, col0 + 2 * nh + h)),
            pl.BlockSpec((t, HEAD_DIM), lambda b, h: (b, col0 + 3 * nh + h)),
            pl.BlockSpec((DN_CONV, HEAD_DIM), lambda b, h: (0, h)),
            pl.BlockSpec((DN_CONV, HEAD_DIM), lambda b, h: (0, nh + h)),
            pl.BlockSpec((DN_CONV, HEAD_DIM), lambda b, h: (0, 2 * nh + h)),
            pl.BlockSpec((t, LANES), lambda b, h: (b, 0)),
            pl.BlockSpec((1, 1, n_chunks, DN_CHUNK), lambda b, h: (b, 2 * nh + h, 0, 0)),
            pl.BlockSpec((1, 1, n_chunks, DN_CHUNK), lambda b, h: (b, 3 * nh + h, 0, 0)),
            pl.BlockSpec((1, HEAD_DIM), lambda b, h: (0, 0)),
        ],
        out_specs=pl.BlockSpec((t, HEAD_DIM), lambda b, h: (b, h)),
        out_shape=jax.ShapeDtypeStruct((n, DN_W), BF16),
        scratch_shapes=[
            pltpu.VMEM((t + 16, HEAD_DIM), F32),
            pltpu.VMEM((t, HEAD_DIM), F32),
            pltpu.VMEM((t, HEAD_DIM), F32),
            pltpu.VMEM((t, HEAD_DIM), F32),
            pltpu.VMEM((2, n_chunks, HEAD_DIM + DN_CHUNK, HEAD_DIM), BF16),
            pltpu.VMEM((2, n_chunks, HEAD_DIM, HEAD_DIM), F32),
            pltpu.VMEM((t, HEAD_DIM), F32),
            pltpu.VMEM((t, HEAD_DIM), F32),
        ],
        compiler_params=_cparams("parallel", "parallel"),
        name="deltanet",
    )(proj, proj, proj, proj, conv_w, conv_w, conv_w, gates_col, gr, gr, out_norm.reshape(1, HEAD_DIM))


def _rope_tables(t):
    inv = ROPE_THETA ** (-np.arange(0, HEAD_DIM, 2, dtype=np.float64) / HEAD_DIM)
    ang = (np.arange(t, dtype=np.float32)[:, None] * inv.astype(np.float32)[None, :]).astype(np.float64)
    cos, sin = np.cos(ang), np.sin(ang)
    cos_full = np.concatenate([cos, cos], axis=1).astype(np.float32)
    sin_signed = np.concatenate([-sin, sin], axis=1).astype(np.float32)
    return jnp.asarray(cos_full), jnp.asarray(sin_signed)


def _rope(x, cos, sin_signed):
    return x * cos + pltpu.roll(x, HEAD_DIM // 2, 1) * sin_signed


def _diff_kernel(q_ref, k_ref, v_ref, gate_ref, cosq_ref, sinq_ref, cosk_ref, sink_ref, qn_ref, kn_ref,
                 lam_ref, sub_ref, o_ref, k_s, e_s, *, lam_init):
    d = HEAD_DIM
    t = k_ref.shape[0]
    tq = q_ref.shape[0]
    n_kc = t // DIFF_KC

    @pl.when(pl.program_id(2) == 0)
    def _():
        for m in range(2):
            k = _rms(k_ref[:, m * d:(m + 1) * d].astype(F32), kn_ref[...])
            k_s[m] = _rope(k, cosk_ref[...], sink_ref[...]).astype(BF16)

    lv = lam_ref[...]
    lam = (jnp.exp(jnp.sum(lv[0:1] * lv[1:2], axis=-1, keepdims=True))
           - jnp.exp(jnp.sum(lv[2:3] * lv[3:4], axis=-1, keepdims=True)) + lam_init)

    def scores(q16, m, c):
        return _dot_nt(q16, k_s[m, c * DIFF_KC:(c + 1) * DIFF_KC, :])

    q16s = []
    for m in range(2):
        q = _rms(q_ref[:, m * d:(m + 1) * d].astype(F32), qn_ref[...])
        q = _rope(q, cosq_ref[...], sinq_ref[...]) * (d ** -0.5 * LOG2E)
        q16s.append(q.astype(BF16))

    g_max = lambda ref: jnp.max(jnp.abs(ref[...]), axis=-1, keepdims=True)
    bound = g_max(qn_ref) * g_max(kn_ref) * (d * d ** -0.5 * LOG2E)

    def exact_max():
        out = []
        for m in range(2):
            mx = jnp.max(scores(q16s[m], m, 0), axis=-1, keepdims=True)
            for c in range(1, n_kc):
                mx = jnp.maximum(mx, jnp.max(scores(q16s[m], m, c), axis=-1, keepdims=True))
            out.append(mx)
        return out

    shifts = lax.cond(bound[0, 0] > DIFF_BOUND_LIMIT, exact_max,
                      lambda: [jnp.broadcast_to(bound, (tq, 1))] * 2)

    def score_pair(c):
        return [scores(q16s[m], m, c) for m in range(2)]

    lsum = [None, None]
    s_next = score_pair(0)
    for c in range(n_kc):
        s_cur = s_next
        if c + 1 < n_kc:
            s_next = score_pair(c + 1)
        for m in range(2):
            e = jnp.exp2(s_cur[m] - shifts[m])
            part = e[:, 0:LANES]
            for j in range(1, DIFF_KC // LANES):
                part = part + e[:, j * LANES:(j + 1) * LANES]
            lsum[m] = part if lsum[m] is None else lsum[m] + part
            e_s[m, :, c * DIFF_KC:(c + 1) * DIFF_KC] = e.astype(BF16)
    l0 = jnp.sum(lsum[0], axis=-1, keepdims=True)
    l1 = jnp.sum(lsum[1], axis=-1, keepdims=True)
    r16 = (lam * l0 / l1).astype(BF16)
    half = tq // 2

    def mix_pair(c):
        cols = slice(c * DIFF_KC, (c + 1) * DIFF_KC)
        return [e_s[0, r * half:(r + 1) * half, cols] - r16[r * half:(r + 1) * half] *
                e_s[1, r * half:(r + 1) * half, cols] for r in range(2)]

    accs = [None, None]
    a_next = mix_pair(0)
    for c in range(n_kc):
        a_cur = a_next
        if c + 1 < n_kc:
            a_next = mix_pair(c + 1)
        for r in range(2):
            pv = _dot(a_cur[r], v_ref[c * DIFF_KC:(c + 1) * DIFF_KC, :])
            accs[r] = pv if accs[r] is None else accs[r] + pv
    o = jnp.concatenate(accs, axis=0) * (1.0 / l0)
    o = _rms(o, sub_ref[...]) * (1.0 - lam_init)
    o_ref[...] = (o * _silu(gate_ref[...].astype(F32))).astype(o_ref.dtype)


def _diff_attention(proj, q_norm, k_norm, lam_vecs, subln, *, batch, layer_idx, tq):
    n = proj.shape[0]
    t = n // batch
    steps = t // tq
    nh = DIFF_HEADS
    w = DIFF_VDIM
    cos, sin = _rope_tables(t)
    lam_init = 0.8 - 0.6 * math.exp(-0.3 * layer_idx)
    const = lambda shape: pl.BlockSpec(shape, lambda b, h, i: (0, 0))
    return pl.pallas_call(
        functools.partial(_diff_kernel, lam_init=lam_init),
        grid=(batch, nh, steps),
        in_specs=[
            pl.BlockSpec((tq, w), lambda b, h, i: (b * steps + i, h)),
            pl.BlockSpec((t, w), lambda b, h, i: (b, nh + h)),
            pl.BlockSpec((t, w), lambda b, h, i: (b, 2 * nh + h)),
            pl.BlockSpec((tq, w), lambda b, h, i: (b * steps + i, 3 * nh + h)),
            pl.BlockSpec((tq, HEAD_DIM), lambda b, h, i: (i, 0)),
            pl.BlockSpec((tq, HEAD_DIM), lambda b, h, i: (i, 0)),
            const((t, HEAD_DIM)),
            const((t, HEAD_DIM)),
            const((1, HEAD_DIM)),
            const((1, HEAD_DIM)),
            const((4, HEAD_DIM)),
            const((1, w)),
        ],
        out_specs=pl.BlockSpec((tq, w), lambda b, h, i: (b * steps + i, h)),
        out_shape=jax.ShapeDtypeStruct((n, DIFF_W), BF16),
        scratch_shapes=[
            pltpu.VMEM((2, t, HEAD_DIM), BF16),
            pltpu.VMEM((2, tq, t), BF16),
        ],
        compiler_params=_cparams("parallel", "parallel", "arbitrary"),
        name="diff_attention",
    )(proj, proj, proj, proj, cos, sin, cos, sin, q_norm.reshape(1, HEAD_DIM), k_norm.reshape(1, HEAD_DIM),
      lam_vecs.astype(F32), subln.reshape(1, w))


def _even_layer(x2d, mem2d, mem_g, ln_g, w_in, w_mem_kv, w_out, na_q_norm, na_k_norm, na_rpb, dn_conv_w,
                dn_a_log, dn_dt_bias, dn_out_norm, mem_q_norm, mem_k_norm, *, batch):
    t = x2d.shape[0] // batch
    ba0 = 4 * NA_W + 4 * DN_W
    ba1 = ba0 + 4 * DN_HEADS
    tn = 2 * MEM_W
    w16 = w_in.astype(BF16)
    w_mem = w16[:, ba1:]
    w_ba = jnp.pad(w16[:, ba0:ba1], ((0, 0), (0, LANES - 4 * DN_HEADS)))
    proj, ba = _proj(x2d, ln_g, [(w16, 0, ba0 // tn), (w_mem, 0, 1)], w_ba, tm=1024, tn=tn, name="even_in_proj")
    kv = _proj(mem2d, mem_g, [(w_mem_kv.astype(BF16), 0, 1)], tm=mem2d.shape[0], tn=tn, name="even_mem_kv")

    bias = _na_bias(na_rpb.astype(F32), t // GRID_W)
    na_o = _na_attention(proj, bias, na_q_norm, na_k_norm, batch=batch)
    g_col, g_row = _dn_gates(ba, dn_a_log, dn_dt_bias, batch=batch)
    dn_o = _deltanet(proj, dn_conv_w.astype(F32), g_col, g_row, dn_out_norm, batch=batch,
                     col0=4 * NA_W // HEAD_DIM)
    mem_blk = ba0 // MEM_W
    mem_o = _mem_attention(proj, mem_blk, mem_blk + 1, kv, mem_q_norm, mem_k_norm, batch=batch, tq=1024,
                           name="even_mem_attention")
    return _outproj(x2d, [na_o, dn_o, mem_o], w_out.astype(BF16), tm=512, name="even_out_proj")


def _odd_layer(x2d, mem2d, mem_g, layer_idx, ln_g, w_in, w_mem_kv, w_out, q_norm, k_norm, lam_vecs, subln_g,
               mem_q_norm, mem_k_norm, *, batch):
    tn = 2 * MEM_W
    proj = _proj(x2d, ln_g, [(w_in.astype(BF16), 0, w_in.shape[1] // tn)], tm=1024, tn=tn, name="odd_in_proj")
    kv = _proj(mem2d, mem_g, [(w_mem_kv.astype(BF16), 0, 1)], tm=mem2d.shape[0], tn=tn, name="odd_mem_kv")
    diff_o = _diff_attention(proj, q_norm, k_norm, lam_vecs, subln_g, batch=batch, layer_idx=layer_idx, tq=512)
    mem_blk = 4 * DIFF_W // MEM_W
    mem_o = _mem_attention(proj, mem_blk, mem_blk + 1, kv, mem_q_norm, mem_k_norm, batch=batch, tq=1024,
                           name="odd_mem_attention")
    return _outproj(x2d, [diff_o, mem_o], w_out.astype(BF16), tm=512, name="odd_out_proj")


def kernel(x, mem, mem_norm_g, e_ln_g, e_w_in, e_w_mem_kv, e_w_out, na_q_norm, na_k_norm, na_rpb, dn_conv_w,
           dn_a_log, dn_dt_bias, dn_out_norm, e_mem_q_norm, e_mem_k_norm, o_ln_g, o_w_in, o_w_mem_kv, o_w_out,
           df_q_norm, df_k_norm, df_lambda, df_subln, o_mem_q_norm, o_mem_k_norm):
    batch, t, d = x.shape
    depth = e_ln_g.shape[0] + o_ln_g.shape[0]
    x2d = x.reshape(batch * t, d)
    mem2d = mem.reshape(-1, d)
    for layer in range(depth):
        i = layer // 2
        if layer % 2 == 0:
            x2d = _even_layer(x2d, mem2d, mem_norm_g, e_ln_g[i], e_w_in[i], e_w_mem_kv[i], e_w_out[i],
                              na_q_norm[i], na_k_norm[i], na_rpb[i], dn_conv_w[i], dn_a_log[i], dn_dt_bias[i],
                              dn_out_norm[i], e_mem_q_norm[i], e_mem_k_norm[i], batch=batch)
        else:
            x2d = _odd_layer(x2d, mem2d, mem_norm_g, layer, o_ln_g[i], o_w_in[i], o_w_mem_kv[i], o_w_out[i],
                             df_q_norm[i], df_k_norm[i], df_lambda[i], df_subln[i], o_mem_q_norm[i],
                             o_mem_k_norm[i], batch=batch)
    return x2d.reshape(batch, t, d)
```

```python
import functools
import math

import numpy as np
import jax
import jax.numpy as jnp
from jax import lax
from jax.experimental import pallas as pl
from jax.experimental.pallas import tpu as pltpu

F32 = jnp.float32
BF16 = jnp.bfloat16

HEAD_DIM = 128
GRID_W = 64
NA_HEADS = 6
NA_WIN_H = 8
NA_WIN_W = 16
DN_HEADS = 6
DN_CONV = 5
DN_CHUNK = 64
DN_UNROLL = 8
MEM_HEADS = 4
DIFF_HEADS = 6
DIFF_VDIM = 2 * HEAD_DIM
DIFF_KC = 512
DIFF_BOUND_LIMIT = 50.0
ROPE_THETA = 10000.0
EPS = 1e-6
LOG2E = math.log2(math.e)

NA_W = NA_HEADS * HEAD_DIM
DN_W = DN_HEADS * HEAD_DIM
MEM_W = MEM_HEADS * HEAD_DIM
DIFF_W = DIFF_HEADS * DIFF_VDIM

LANES = 128
VMEM_LIMIT = 56 * 1024 * 1024

NA_QROWS = 4
NA_KROWS = 12
NA_TQ = NA_QROWS * GRID_W
NA_TK = NA_KROWS * GRID_W


def _cparams(*sem):
    return pltpu.CompilerParams(dimension_semantics=sem, vmem_limit_bytes=VMEM_LIMIT)


def _dot(a, b):
    return jnp.dot(a, b, preferred_element_type=F32)


def _dot_nt(a, b):
    return lax.dot_general(a, b, (((1,), (1,)), ((), ())), preferred_element_type=F32)


def _silu(x):
    return x * (0.5 * jnp.tanh(0.5 * x) + 0.5)


def _rowsum_mxu(x):
    ones = jnp.ones((LANES, LANES), BF16)
    hi = x.astype(BF16)
    lo = (x - hi.astype(F32)).astype(BF16)
    return _dot(hi, ones) + _dot(lo, ones)


def _rms(x, g):
    return x * lax.rsqrt(jnp.mean(x * x, axis=-1, keepdims=True) + EPS) * g


def _proj_kernel(x_ref, g_ref, *rest, tiles, has_aux):
    w_refs = rest[:len(tiles)]
    rest = rest[len(tiles):]
    if has_aux:
        waux_ref, o_ref, aux_ref, h_ref = rest
    else:
        o_ref, h_ref = rest
    j = pl.program_id(1)

    @pl.when(j == 0)
    def _():
        h = _rms(x_ref[...], g_ref[...]).astype(BF16)
        h_ref[...] = h
        if has_aux:
            aux_ref[...] = _dot(h, waux_ref[...])

    first = 0
    for w_ref, n_tiles in zip(w_refs, tiles):
        @pl.when((j >= first) & (j < first + n_tiles))
        def _(w_ref=w_ref):
            o_ref[...] = _dot(h_ref[...], w_ref[...]).astype(o_ref.dtype)
        first += n_tiles


def _proj(x2d, g, w_groups, w_aux=None, *, tm, tn, name):
    n, d = x2d.shape
    tiles = tuple(nt for _, _, nt in w_groups)
    nc = sum(tiles) * tn
    has_aux = w_aux is not None
    in_specs = [
        pl.BlockSpec((tm, d), lambda i, j: (i, 0)),
        pl.BlockSpec((1, d), lambda i, j: (0, 0)),
    ]
    first = 0
    for _, tile0, nt in w_groups:
        in_specs.append(pl.BlockSpec(
            (d, tn), lambda i, j, first=first, tile0=tile0, nt=nt: (0, tile0 + jnp.clip(j - first, 0, nt - 1))))
        first += nt
    out_shape = [jax.ShapeDtypeStruct((n, nc), BF16)]
    out_specs = [pl.BlockSpec((tm, tn), lambda i, j: (i, j))]
    args = [x2d, g.reshape(1, d)] + [w for w, _, _ in w_groups]
    if has_aux:
        in_specs.append(pl.BlockSpec((d, LANES), lambda i, j: (0, 0)))
        out_shape.append(jax.ShapeDtypeStruct((n, LANES), F32))
        out_specs.append(pl.BlockSpec((tm, LANES), lambda i, j: (i, 0)))
        args.append(w_aux)
    res = pl.pallas_call(
        functools.partial(_proj_kernel, tiles=tiles, has_aux=has_aux),
        grid=(n // tm, nc // tn),
        in_specs=in_specs,
        out_specs=out_specs,
        out_shape=out_shape,
        scratch_shapes=[pltpu.VMEM((tm, d), BF16)],
        compiler_params=_cparams("parallel", "arbitrary"),
        name=name,
    )(*args)
    return res if has_aux else res[0]


def _outproj_kernel(*refs, n_in):
    x_ref = refs[0]
    a_refs = refs[1:1 + n_in]
    w_refs = refs[1 + n_in:1 + 2 * n_in]
    o_ref = refs[1 + 2 * n_in]
    acc = x_ref[...]
    for a_ref, w_ref in zip(a_refs, w_refs):
        acc = acc + _dot(a_ref[...], w_ref[...])
    o_ref[...] = acc


def _outproj(x2d, acts, w, *, tm, name):
    n, d = x2d.shape
    n_in = len(acts)
    in_specs = [pl.BlockSpec((tm, d), lambda i: (i, 0))]
    in_specs += [pl.BlockSpec((tm, a.shape[1]), lambda i: (i, 0)) for a in acts]
    row0 = 0
    for a in acts:
        rows = a.shape[1]
        assert row0 % rows == 0, "a weight row group must start at a multiple of its height"
        in_specs.append(pl.BlockSpec((rows, d), lambda i, blk=row0 // rows: (blk, 0)))
        row0 += rows
    weights = [w] * n_in
    return pl.pallas_call(
        functools.partial(_outproj_kernel, n_in=n_in),
        grid=(n // tm,),
        in_specs=in_specs,
        out_specs=pl.BlockSpec((tm, d), lambda i: (i, 0)),
        out_shape=jax.ShapeDtypeStruct((n, d), F32),
        compiler_params=_cparams("parallel"),
        name=name,
    )(x2d, *acts, *weights)


def _mem_kernel(q_ref, gate_ref, k_ref, v_ref, qn_ref, kn_ref, o_ref):
    scale = HEAD_DIM ** -0.5
    for h in range(MEM_HEADS):
        cols = slice(h * HEAD_DIM, (h + 1) * HEAD_DIM)
        q = (_rms(q_ref[:, cols].astype(F32), qn_ref[...]) * scale).astype(BF16)
        k = _rms(k_ref[:, cols].astype(F32), kn_ref[...]).astype(BF16)
        s = _dot_nt(q, k)
        e = jnp.exp(s - jnp.max(s, axis=-1, keepdims=True))
        p = (e * (1.0 / jnp.sum(e, axis=-1, keepdims=True))).astype(BF16)
        o = _dot(p, v_ref[:, cols])
        o_ref[:, cols] = (o * _silu(gate_ref[:, cols].astype(F32))).astype(o_ref.dtype)


def _mem_attention(proj, q_blk, gate_blk, kv, q_norm, k_norm, *, batch, tq, name):
    n = proj.shape[0]
    t = n // batch
    m = kv.shape[0] // batch
    steps = t // tq
    return pl.pallas_call(
        _mem_kernel,
        grid=(batch, steps),
        in_specs=[
            pl.BlockSpec((tq, MEM_W), lambda b, i: (b * steps + i, q_blk)),
            pl.BlockSpec((tq, MEM_W), lambda b, i: (b * steps + i, gate_blk)),
            pl.BlockSpec((m, MEM_W), lambda b, i: (b, 0)),
            pl.BlockSpec((m, MEM_W), lambda b, i: (b, 1)),
            pl.BlockSpec((1, HEAD_DIM), lambda b, i: (0, 0)),
            pl.BlockSpec((1, HEAD_DIM), lambda b, i: (0, 0)),
        ],
        out_specs=pl.BlockSpec((tq, MEM_W), lambda b, i: (b * steps + i, 0)),
        out_shape=jax.ShapeDtypeStruct((n, MEM_W), BF16),
        compiler_params=_cparams("parallel", "parallel"),
        name=name,
    )(proj, proj, kv, kv, q_norm.reshape(1, HEAD_DIM), k_norm.reshape(1, HEAD_DIM))


def _na_bias_kernel(rpb_ref, o_ref, *, n_rows):
    h = pl.program_id(0)
    n_dc = 2 * NA_WIN_W - 1
    qc = lax.broadcasted_iota(jnp.int32, (GRID_W, GRID_W), 0)
    kc = lax.broadcasted_iota(jnp.int32, (GRID_W, GRID_W), 1)
    dcm = jnp.clip(kc - qc, -(NA_WIN_W - 1), NA_WIN_W - 1) + (NA_WIN_W - 1)
    c0 = jnp.clip(qc - NA_WIN_W // 2, 0, GRID_W - NA_WIN_W)
    col_ok = (kc >= c0) & (kc < c0 + NA_WIN_W)
    neg = jnp.full((GRID_W, GRID_W), -jnp.inf, F32)
    tiles = []
    for dr in range(2 * NA_WIN_H - 1):
        t = jnp.zeros((GRID_W, GRID_W), F32)
        for dc in range(n_dc):
            t = jnp.where(dcm == dc, rpb_ref[h, dr * n_dc + dc], t)
        tiles.append(jnp.where(col_ok, t * LOG2E, neg))
    n_blocks = n_rows // NA_QROWS
    for case, j in enumerate((0, 1, n_blocks - 1)):
        start = min(max(NA_QROWS * j - NA_QROWS, 0), n_rows - NA_KROWS)
        for a in range(NA_QROWS):
            qr = NA_QROWS * j + a
            r0 = min(max(qr - NA_WIN_H // 2, 0), n_rows - NA_WIN_H)
            for b in range(NA_KROWS):
                kr = start + b
                tile = tiles[kr - qr + NA_WIN_H - 1] if r0 <= kr < r0 + NA_WIN_H else neg
                o_ref[0, case, a * GRID_W:(a + 1) * GRID_W, b * GRID_W:(b + 1) * GRID_W] = tile


def _na_bias(rpb, n_rows):
    h = rpb.shape[0]
    return pl.pallas_call(
        functools.partial(_na_bias_kernel, n_rows=n_rows),
        grid=(h,),
        in_specs=[pl.BlockSpec(memory_space=pltpu.SMEM)],
        out_specs=pl.BlockSpec((1, 3, NA_TQ, NA_TK), lambda i: (i, 0, 0, 0)),
        out_shape=jax.ShapeDtypeStruct((h, 3, NA_TQ, NA_TK), F32),
        compiler_params=_cparams("parallel"),
        name="na_bias",
    )(rpb.reshape(h, -1))


def _na_kernel(q_ref, k_ref, v_ref, gate_ref, bias_ref, qn_ref, kn_ref, o_ref, k_s, *, n_blocks):
    k_s[...] = _rms(k_ref[...].astype(F32), kn_ref[...]).astype(BF16)

    def key_start(j):
        return min(max(j - 1, 0), n_blocks - NA_KROWS // NA_QROWS) * NA_TQ

    def scores(j):
        q = _rms(q_ref[j * NA_TQ:(j + 1) * NA_TQ, :].astype(F32), qn_ref[...])
        q16 = (q * (HEAD_DIM ** -0.5 * LOG2E)).astype(BF16)
        case = 0 if j == 0 else (2 if j == n_blocks - 1 else 1)
        return _dot_nt(q16, k_s[key_start(j):key_start(j) + NA_TK, :]) + bias_ref[0, case]

    s_next = scores(0)
    for j in range(n_blocks):
        s = s_next
        if j + 1 < n_blocks:
            s_next = scores(j + 1)
        rows = slice(j * NA_TQ, (j + 1) * NA_TQ)
        e = jnp.exp2(s - jnp.max(s, axis=-1, keepdims=True))
        inv_l = 1.0 / jnp.sum(e, axis=-1, keepdims=True)
        o = _dot(e.astype(BF16), v_ref[key_start(j):key_start(j) + NA_TK, :]) * inv_l
        o_ref[rows, :] = (o * _silu(gate_ref[rows, :].astype(F32))).astype(o_ref.dtype)


def _na_attention(proj, bias, q_norm, k_norm, *, batch):
    n = proj.shape[0]
    t = n // batch
    n_blocks = t // NA_TQ
    nh = NA_HEADS
    head_block = lambda off: pl.BlockSpec((t, HEAD_DIM), lambda b, h: (b, off + h))
    return pl.pallas_call(
        functools.partial(_na_kernel, n_blocks=n_blocks),
        grid=(batch, nh),
        in_specs=[
            head_block(0),
            head_block(nh),
            head_block(2 * nh),
            head_block(3 * nh),
            pl.BlockSpec((1, 3, NA_TQ, NA_TK), lambda b, h: (h, 0, 0, 0)),
            pl.BlockSpec((1, HEAD_DIM), lambda b, h: (0, 0)),
            pl.BlockSpec((1, HEAD_DIM), lambda b, h: (0, 0)),
        ],
        out_specs=head_block(0),
        out_shape=jax.ShapeDtypeStruct((n, NA_W), BF16),
        scratch_shapes=[pltpu.VMEM((t, HEAD_DIM), BF16)],
        compiler_params=_cparams("parallel", "parallel"),
        name="na_attention",
    )(proj, proj, proj, proj, bias, q_norm.reshape(1, HEAD_DIM), k_norm.reshape(1, HEAD_DIM))


def _split3(x):
    hi = x.astype(BF16)
    r = x - hi.astype(F32)
    mid = r.astype(BF16)
    lo = (r - mid.astype(F32)).astype(BF16)
    return hi, mid, lo


def _dn_gates_kernel(ba_ref, alog_ref, dtb_ref, col_ref, row_ref, mask_s):
    tm = ba_ref.shape[0]
    ba = ba_ref[...]
    lane = lax.broadcasted_iota(jnp.int32, (1, LANES), 1)
    beta = 1.0 / (1.0 + jnp.exp(-ba))
    z = ba + dtb_ref[...]
    softplus = jnp.maximum(z, 0.0) + jnp.log(1.0 + jnp.exp(-jnp.abs(z)))
    g = -jnp.exp(alog_ref[...]) * softplus

    @pl.when((pl.program_id(0) == 0) & (pl.program_id(1) == 0))
    def _():
        r = lax.broadcasted_iota(jnp.int32, (tm, tm), 0)
        c = lax.broadcasted_iota(jnp.int32, (tm, tm), 1)
        same = (r // DN_CHUNK) == (c // DN_CHUNK)
        mask_s[0] = jnp.where(same & (c <= r), 1.0, 0.0).astype(BF16)
        mask_s[1] = jnp.where(same & (c >= r), 1.0, 0.0).astype(BF16)

    parts = _split3(g)
    m_f = mask_s[0]
    m_b = mask_s[1]
    g_f = _dot(m_f, parts[0]) + _dot(m_f, parts[1]) + _dot(m_f, parts[2])
    g_b = _dot(m_b, parts[0]) + _dot(m_b, parts[1]) + _dot(m_b, parts[2])
    nh = DN_HEADS
    out = jnp.where(lane < 2 * nh, beta, jnp.where(lane < 3 * nh, g_f, jnp.where(lane < 4 * nh, g_b, 0.0)))
    col_ref[...] = out
    for chunk in range(tm // DN_CHUNK):
        slab = out[chunk * DN_CHUNK:(chunk + 1) * DN_CHUNK, :].T
        row_ref[0, :, chunk, :] = jnp.concatenate([slab, slab], axis=1)


def _dn_gates(ba, a_log, dt_bias, *, batch, tm=512):
    n = ba.shape[0]
    t = n // batch
    steps = t // tm
    pad = jnp.zeros((2 * DN_HEADS,), F32)
    alog_row = jnp.concatenate([pad, a_log.reshape(-1).astype(F32), jnp.zeros((LANES - 4 * DN_HEADS,), F32)])
    dtb_row = jnp.concatenate([pad, dt_bias.reshape(-1).astype(F32), jnp.zeros((LANES - 4 * DN_HEADS,), F32)])
    return pl.pallas_call(
        _dn_gates_kernel,
        grid=(batch, steps),
        in_specs=[
            pl.BlockSpec((tm, LANES), lambda b, i: (b * steps + i, 0)),
            pl.BlockSpec((1, LANES), lambda b, i: (0, 0)),
            pl.BlockSpec((1, LANES), lambda b, i: (0, 0)),
        ],
        out_specs=[
            pl.BlockSpec((tm, LANES), lambda b, i: (b * steps + i, 0)),
            pl.BlockSpec((1, LANES, tm // DN_CHUNK, 2 * DN_CHUNK), lambda b, i: (b, 0, i, 0)),
        ],
        out_shape=[
            jax.ShapeDtypeStruct((n, LANES), F32),
            jax.ShapeDtypeStruct((batch, LANES, t // DN_CHUNK, 2 * DN_CHUNK), F32),
        ],
        scratch_shapes=[pltpu.VMEM((2, tm, tm), BF16)],
        compiler_params=_cparams("arbitrary", "arbitrary"),
        name="dn_gates",
    )(ba, alog_row.reshape(1, LANES), dtb_row.reshape(1, LANES))


def _dn_kernel(q_ref, k_ref, v_ref, gate_ref, cwq_ref, cwk_ref, cwv_ref, gc_ref, grf_ref, grb_ref,
               on_ref, o_ref, xp_s, qn_s, kn_s, vn_s, mq_s, n_s, of_s, ob_s):
    t = q_ref.shape[0]
    c_sz = DN_CHUNK
    n_chunks = t // c_sz
    head = pl.program_id(1)

    border = jnp.zeros((8, HEAD_DIM), F32)
    xp_s[0:8, :] = border
    xp_s[t + 8:t + 16, :] = border

    def conv_silu(x_ref, cw_ref):
        xp_s[8:t + 8, :] = x_ref[...].astype(F32)
        acc = None
        for tap in range(DN_CONV):
            start = 8 + tap - DN_CONV // 2
            term = xp_s[start:start + t, :] * cw_ref[tap:tap + 1, :]
            acc = term if acc is None else acc + term
        return _silu(acc)

    def l2n(x):
        return x * lax.rsqrt(_rowsum_mxu(x * x) + EPS)

    qn_s[...] = l2n(conv_silu(q_ref, cwq_ref)) * HEAD_DIM ** -0.5
    kn_s[...] = l2n(conv_silu(k_ref, cwk_ref))
    vn_s[...] = conv_silu(v_ref, cwv_ref)

    lane = lax.broadcasted_iota(jnp.int32, (1, LANES), 1)
    ii = lax.broadcasted_iota(jnp.int32, (c_sz, 2 * c_sz), 0)
    lane2 = lax.broadcasted_iota(jnp.int32, (c_sz, 2 * c_sz), 1)
    jj = lane2 & (c_sz - 1)
    right = lane2 >= c_sz
    eye_left = jnp.where(ii == lane2, 1.0, 0.0)
    zero_rows = jnp.zeros((c_sz, 2 * c_sz), BF16)

    n_levels = int(math.log2(c_sz))

    def local_body(i, carry):
        chains = []
        for u in range(DN_UNROLL):
            c = i * DN_UNROLL + u
            rows = pl.ds(pl.multiple_of(c * c_sz, c_sz), c_sz)
            kc = kn_s[rows, :]
            qc = qn_s[rows, :]
            vc = vn_s[rows, :]
            gcb = gc_ref[rows, :]
            k16 = kc.astype(BF16)
            kk = _dot_nt(k16, jnp.concatenate([k16, k16], axis=0))
            qk_raw = _dot_nt(qc.astype(BF16), k16)
            for d, gr_ref in enumerate((grf_ref, grb_ref)):
                beta = jnp.sum(jnp.where(lane == d * DN_HEADS + head, gcb, 0.0), axis=1, keepdims=True)
                g_col = jnp.sum(jnp.where(lane == (2 + d) * DN_HEADS + head, gcb, 0.0), axis=1, keepdims=True)
                g_row = gr_ref[0, 0, pl.ds(c, 1), :]
                incl = (ii >= jj) if d == 0 else (ii <= jj)
                strict = (ii > jj) if d == 0 else (ii < jj)
                g_edge = g_row[:, c_sz - 1:c_sz] if d == 0 else g_row[:, 0:1]
                decay = jnp.where(incl, jnp.exp(jnp.where(incl, g_col - g_row, 0.0)), 0.0)
                e_g = jnp.exp(g_col)
                q_dec = qc * e_g
                kd = kc * jnp.exp(g_edge - g_col)
                qk = qk_raw * decay[:, :c_sz]
                lhs = jnp.concatenate([kd.T.astype(BF16), qk.astype(BF16)], axis=0)
                z = jnp.where(right & strict, -(beta * kk * decay), 0.0)
                rhs = jnp.concatenate([vc * beta, kc * (beta * e_g)], axis=1)
                chains.append([d, c, rows, None, z, lhs, q_dec, rhs])
        for lvl in range(n_levels):
            outs = []
            for _, _, _, _, z, _, _, _ in chains:
                w = jnp.concatenate([zero_rows, (z + eye_left).astype(BF16)], axis=0)
                outs.append(_dot(z.astype(BF16), w))
            for chain, out in zip(chains, outs):
                chain[4] = jnp.where(right, 0.0, chain[4]) + out
        sols = [_dot(z.astype(BF16), jnp.concatenate([rhs.astype(BF16), jnp.zeros_like(rhs, BF16)], axis=0))
                for _, _, _, _, z, _, _, rhs in chains]
        for chain, s in zip(chains, sols):
            chain[4] = chain[7] + s
            del chain[7]
        prods = [_dot(lhs, sol.astype(BF16)) for _, _, _, _, sol, lhs, _ in chains]
        for (d, c, rows, _, _, _, q_dec), x in zip(chains, prods):
            n_s[d, c] = x[:HEAD_DIM, :HEAD_DIM]
            mq_s[d, c, 0:HEAD_DIM, :] = x[:HEAD_DIM, HEAD_DIM:].astype(BF16)
            mq_s[d, c, HEAD_DIM:HEAD_DIM + c_sz, :] = (q_dec - x[HEAD_DIM:, HEAD_DIM:]).astype(BF16)
            (of_s, ob_s)[d][rows, :] = x[HEAD_DIM:, :HEAD_DIM]
        return carry

    lax.fori_loop(0, n_chunks // DN_UNROLL, local_body, 0)

    def scan_body(i, states):
        cs = (i, n_chunks - 1 - i)
        ys = [_dot(mq_s[d, cs[d]], states[d].astype(BF16)) for d in range(2)]
        new_states = []
        for d, (gr_ref, o_s) in enumerate(((grf_ref, of_s), (grb_ref, ob_s))):
            rows = pl.ds(pl.multiple_of(cs[d] * c_sz, c_sz), c_sz)
            g_row = gr_ref[0, 0, pl.ds(cs[d], 1), :]
            g_edge = g_row[:, c_sz - 1:c_sz] if d == 0 else g_row[:, 0:1]
            o_s[rows, :] = o_s[rows, :] + ys[d][HEAD_DIM:]
            new_states.append(states[d] * jnp.exp(g_edge) + n_s[d, cs[d]] - ys[d][:HEAD_DIM])
        return tuple(new_states)

    zero = jnp.zeros((HEAD_DIM, HEAD_DIM), F32)
    lax.fori_loop(0, n_chunks, scan_body, (zero, zero))

    o = of_s[...] + ob_s[...]
    o = o * lax.rsqrt(_rowsum_mxu(o * o) * (1.0 / HEAD_DIM) + EPS) * on_ref[...]
    o_ref[...] = (o * _silu(gate_ref[...].astype(F32))).astype(o_ref.dtype)


def _deltanet(proj, conv_w, gates_col, gates_row, out_norm, *, batch, col0):
    n = proj.shape[0]
    t = n // batch
    n_chunks = t // DN_CHUNK
    nh = DN_HEADS
    gr = gates_row
    return pl.pallas_call(
        _dn_kernel,
        grid=(batch, nh),
        in_specs=[
            pl.BlockSpec((t, HEAD_DIM), lambda b, h: (b, col0 + h)),
            pl.BlockSpec((t, HEAD_DIM), lambda b, h: (b, col0 + nh + h)),
            pl.BlockSpec((t, HEAD_DIM), lambda b, h: (b, col0 + 2 * nh + h)),
            pl.BlockSpec((t, HEAD_DIM), lambda b, h: (b, col0 + 3 * nh + h)),
            pl.BlockSpec((DN_CONV, HEAD_DIM), lambda b, h: (0, h)),
            pl.BlockSpec((DN_CONV, HEAD_DIM), lambda b, h: (0, nh + h)),
            pl.BlockSpec((DN_CONV, HEAD_DIM), lambda b, h: (0, 2 * nh + h)),
            pl.BlockSpec((t, LANES), lambda b, h: (b, 0)),
            pl.BlockSpec((1, 1, n_chunks, 2 * DN_CHUNK), lambda b, h: (b, 2 * nh + h, 0, 0)),
            pl.BlockSpec((1, 1, n_chunks, 2 * DN_CHUNK), lambda b, h: (b, 3 * nh + h, 0, 0)),
            pl.BlockSpec((1, HEAD_DIM), lambda b, h: (0, 0)),
        ],
        out_specs=pl.BlockSpec((t, HEAD_DIM), lambda b, h: (b, h)),
        out_shape=jax.ShapeDtypeStruct((n, DN_W), BF16),
        scratch_shapes=[
            pltpu.VMEM((t + 16, HEAD_DIM), F32),
            pltpu.VMEM((t, HEAD_DIM), F32),
            pltpu.VMEM((t, HEAD_DIM), F32),
            pltpu.VMEM((t, HEAD_DIM), F32),
            pltpu.VMEM((2, n_chunks, HEAD_DIM + DN_CHUNK, HEAD_DIM), BF16),
            pltpu.VMEM((2, n_chunks, HEAD_DIM, HEAD_DIM), F32),
            pltpu.VMEM((t, HEAD_DIM), F32),
            pltpu.VMEM((t, HEAD_DIM), F32),
        ],
        compiler_params=_cparams("parallel", "parallel"),
        name="deltanet",
    )(proj, proj, proj, proj, conv_w, conv_w, conv_w, gates_col, gr, gr, out_norm.reshape(1, HEAD_DIM))


def _rope_tables(t):
    inv = ROPE_THETA ** (-np.arange(0, HEAD_DIM, 2, dtype=np.float64) / HEAD_DIM)
    ang = (np.arange(t, dtype=np.float32)[:, None] * inv.astype(np.float32)[None, :]).astype(np.float64)
    cos, sin = np.cos(ang), np.sin(ang)
    cos_full = np.concatenate([cos, cos], axis=1).astype(np.float32)
    sin_signed = np.concatenate([-sin, sin], axis=1).astype(np.float32)
    return jnp.asarray(cos_full), jnp.asarray(sin_signed)


def _rope(x, cos, sin_signed):
    return x * cos + pltpu.roll(x, HEAD_DIM // 2, 1) * sin_signed


def _diff_kernel(q_ref, k_ref, v_ref, gate_ref, cosq_ref, sinq_ref, cosk_ref, sink_ref, qn_ref, kn_ref,
                 lam_ref, sub_ref, o_ref, k_s, e_s, *, lam_init):
    d = HEAD_DIM
    t = k_ref.shape[0]
    tq = q_ref.shape[0]
    n_kc = t // DIFF_KC

    @pl.when(pl.program_id(2) == 0)
    def _():
        for m in range(2):
            k = _rms(k_ref[:, m * d:(m + 1) * d].astype(F32), kn_ref[...])
            k_s[m] = _rope(k, cosk_ref[...], sink_ref[...]).astype(BF16)

    lv = lam_ref[...]
    lam = (jnp.exp(jnp.sum(lv[0:1] * lv[1:2], axis=-1, keepdims=True))
           - jnp.exp(jnp.sum(lv[2:3] * lv[3:4], axis=-1, keepdims=True)) + lam_init)

    def scores(q16, m, c):
        return _dot_nt(q16, k_s[m, c * DIFF_KC:(c + 1) * DIFF_KC, :])

    q16s = []
    for m in range(2):
        q = _rms(q_ref[:, m * d:(m + 1) * d].astype(F32), qn_ref[...])
        q = _rope(q, cosq_ref[...], sinq_ref[...]) * (d ** -0.5 * LOG2E)
        q16s.append(q.astype(BF16))

    g_max = lambda ref: jnp.max(jnp.abs(ref[...]), axis=-1, keepdims=True)
    bound = g_max(qn_ref) * g_max(kn_ref) * (d * d ** -0.5 * LOG2E)

    def exact_max():
        out = []
        for m in range(2):
            mx = jnp.max(scores(q16s[m], m, 0), axis=-1, keepdims=True)
            for c in range(1, n_kc):
                mx = jnp.maximum(mx, jnp.max(scores(q16s[m], m, c), axis=-1, keepdims=True))
            out.append(mx)
        return out

    shifts = lax.cond(bound[0, 0] > DIFF_BOUND_LIMIT, exact_max,
                      lambda: [jnp.broadcast_to(bound, (tq, 1))] * 2)

    def score_pair(c):
        return [scores(q16s[m], m, c) for m in range(2)]

    lsum = [None, None]
    s_next = score_pair(0)
    for c in range(n_kc):
        s_cur = s_next
        if c + 1 < n_kc:
            s_next = score_pair(c + 1)
        for m in range(2):
            e = jnp.exp2(s_cur[m] - shifts[m])
            part = e[:, 0:LANES]
            for j in range(1, DIFF_KC // LANES):
                part = part + e[:, j * LANES:(j + 1) * LANES]
            lsum[m] = part if lsum[m] is None else lsum[m] + part
            e_s[m, :, c * DIFF_KC:(c + 1) * DIFF_KC] = e.astype(BF16)
    l0 = jnp.sum(lsum[0], axis=-1, keepdims=True)
    l1 = jnp.sum(lsum[1], axis=-1, keepdims=True)
    r16 = (lam * l0 / l1).astype(BF16)
    half = tq // 2

    def mix_pair(c):
        cols = slice(c * DIFF_KC, (c + 1) * DIFF_KC)
        return [e_s[0, r * half:(r + 1) * half, cols] - r16[r * half:(r + 1) * half] *
                e_s[1, r * half:(r + 1) * half, cols] for r in range(2)]

    accs = [None, None]
    a_next = mix_pair(0)
    for c in range(n_kc):
        a_cur = a_next
        if c + 1 < n_kc:
            a_next = mix_pair(c + 1)
        for r in range(2):
            pv = _dot(a_cur[r], v_ref[c * DIFF_KC:(c + 1) * DIFF_KC, :])
            accs[r] = pv if accs[r] is None else accs[r] + pv
    o = jnp.concatenate(accs, axis=0) * (1.0 / l0)
    o = _rms(o, sub_ref[...]) * (1.0 - lam_init)
    o_ref[...] = (o * _silu(gate_ref[...].astype(F32))).astype(o_ref.dtype)


def _diff_attention(proj, q_norm, k_norm, lam_vecs, subln, *, batch, layer_idx, tq):
    n = proj.shape[0]
    t = n // batch
    steps = t // tq
    nh = DIFF_HEADS
    w = DIFF_VDIM
    cos, sin = _rope_tables(t)
    lam_init = 0.8 - 0.6 * math.exp(-0.3 * layer_idx)
    const = lambda shape: pl.BlockSpec(shape, lambda b, h, i: (0, 0))
    return pl.pallas_call(
        functools.partial(_diff_kernel, lam_init=lam_init),
        grid=(batch, nh, steps),
        in_specs=[
            pl.BlockSpec((tq, w), lambda b, h, i: (b * steps + i, h)),
            pl.BlockSpec((t, w), lambda b, h, i: (b, nh + h)),
            pl.BlockSpec((t, w), lambda b, h, i: (b, 2 * nh + h)),
            pl.BlockSpec((tq, w), lambda b, h, i: (b * steps + i, 3 * nh + h)),
            pl.BlockSpec((tq, HEAD_DIM), lambda b, h, i: (i, 0)),
            pl.BlockSpec((tq, HEAD_DIM), lambda b, h, i: (i, 0)),
            const((t, HEAD_DIM)),
            const((t, HEAD_DIM)),
            const((1, HEAD_DIM)),
            const((1, HEAD_DIM)),
            const((4, HEAD_DIM)),
            const((1, w)),
        ],
        out_specs=pl.BlockSpec((tq, w), lambda b, h, i: (b * steps + i, h)),
        out_shape=jax.ShapeDtypeStruct((n, DIFF_W), BF16),
        scratch_shapes=[
            pltpu.VMEM((2, t, HEAD_DIM), BF16),
            pltpu.VMEM((2, tq, t), BF16),
        ],
        compiler_params=_cparams("parallel", "parallel", "arbitrary"),
        name="diff_attention",
    )(proj, proj, proj, proj, cos, sin, cos, sin, q_norm.reshape(1, HEAD_DIM), k_norm.reshape(1, HEAD_DIM),
      lam_vecs.astype(F32), subln.reshape(1, w))


def _even_layer(x2d, mem2d, mem_g, ln_g, w_in, w_mem_kv, w_out, na_q_norm, na_k_norm, na_rpb, dn_conv_w,
                dn_a_log, dn_dt_bias, dn_out_norm, mem_q_norm, mem_k_norm, *, batch):
    t = x2d.shape[0] // batch
    ba0 = 4 * NA_W + 4 * DN_W
    ba1 = ba0 + 4 * DN_HEADS
    tn = 2 * MEM_W
    w16 = w_in.astype(BF16)
    w_mem = w16[:, ba1:]
    w_ba = jnp.pad(w16[:, ba0:ba1], ((0, 0), (0, LANES - 4 * DN_HEADS)))
    proj, ba = _proj(x2d, ln_g, [(w16, 0, ba0 // tn), (w_mem, 0, 1)], w_ba, tm=1024, tn=tn, name="even_in_proj")
    kv = _proj(mem2d, mem_g, [(w_mem_kv.astype(BF16), 0, 1)], tm=mem2d.shape[0], tn=tn, name="even_mem_kv")

    bias = _na_bias(na_rpb.astype(F32), t // GRID_W)
    na_o = _na_attention(proj, bias, na_q_norm, na_k_norm, batch=batch)
    g_col, g_row = _dn_gates(ba, dn_a_log, dn_dt_bias, batch=batch)
    dn_o = _deltanet(proj, dn_conv_w.astype(F32), g_col, g_row, dn_out_norm, batch=batch,
                     col0=4 * NA_W // HEAD_DIM)
    mem_blk = ba0 // MEM_W
    mem_o = _mem_attention(proj, mem_blk, mem_blk + 1, kv, mem_q_norm, mem_k_norm, batch=batch, tq=1024,
                           name="even_mem_attention")
    return _outproj(x2d, [na_o, dn_o, mem_o], w_out.astype(BF16), tm=512, name="even_out_proj")


def _odd_layer(x2d, mem2d, mem_g, layer_idx, ln_g, w_in, w_mem_kv, w_out, q_norm, k_norm, lam_vecs, subln_g,
               mem_q_norm, mem_k_norm, *, batch):
    tn = 2 * MEM_W
    proj = _proj(x2d, ln_g, [(w_in.astype(BF16), 0, w_in.shape[1] // tn)], tm=1024, tn=tn, name="odd_in_proj")
    kv = _proj(mem2d, mem_g, [(w_mem_kv.astype(BF16), 0, 1)], tm=mem2d.shape[0], tn=tn, name="odd_mem_kv")
    diff_o = _diff_attention(proj, q_norm, k_norm, lam_vecs, subln_g, batch=batch, layer_idx=layer_idx, tq=1024)
    mem_blk = 4 * DIFF_W // MEM_W
    mem_o = _mem_attention(proj, mem_blk, mem_blk + 1, kv, mem_q_norm, mem_k_norm, batch=batch, tq=1024,
                           name="odd_mem_attention")
    return _outproj(x2d, [diff_o, mem_o], w_out.astype(BF16), tm=512, name="odd_out_proj")


def kernel(x, mem, mem_norm_g, e_ln_g, e_w_in, e_w_mem_kv, e_w_out, na_q_norm, na_k_norm, na_rpb, dn_conv_w,
           dn_a_log, dn_dt_bias, dn_out_norm, e_mem_q_norm, e_mem_k_norm, o_ln_g, o_w_in, o_w_mem_kv, o_w_out,
           df_q_norm, df_k_norm, df_lambda, df_subln, o_mem_q_norm, o_mem_k_norm):
    batch, t, d = x.shape
    depth = e_ln_g.shape[0] + o_ln_g.shape[0]
    x2d = x.reshape(batch * t, d)
    mem2d = mem.reshape(-1, d)
    for layer in range(depth):
        i = layer // 2
        if layer % 2 == 0:
            x2d = _even_layer(x2d, mem2d, mem_norm_g, e_ln_g[i], e_w_in[i], e_w_mem_kv[i], e_w_out[i],
                              na_q_norm[i], na_k_norm[i], na_rpb[i], dn_conv_w[i], dn_a_log[i], dn_dt_bias[i],
                              dn_out_norm[i], e_mem_q_norm[i], e_mem_k_norm[i], batch=batch)
        else:
            x2d = _odd_layer(x2d, mem2d, mem_norm_g, layer, o_ln_g[i], o_w_in[i], o_w_mem_kv[i], o_w_out[i],
                             df_q_norm[i], df_k_norm[i], df_lambda[i], df_subln[i], o_mem_q_norm[i],
                             o_mem_k_norm[i], batch=batch)
    return x2d.reshape(batch, t, d)
```

```python
import functools
import math

import numpy as np
import jax
import jax.numpy as jnp
from jax import lax
from jax.experimental import pallas as pl
from jax.experimental.pallas import tpu as pltpu

F32 = jnp.float32
BF16 = jnp.bfloat16

HEAD_DIM = 128
GRID_W = 64
NA_HEADS = 6
NA_WIN_H = 8
NA_WIN_W = 16
DN_HEADS = 6
DN_CONV = 5
DN_CHUNK = 64
DN_UNROLL = 8
MEM_HEADS = 4
DIFF_HEADS = 6
DIFF_VDIM = 2 * HEAD_DIM
DIFF_KC = 512
DIFF_BOUND_LIMIT = 50.0
ROPE_THETA = 10000.0
EPS = 1e-6
LOG2E = math.log2(math.e)

NA_W = NA_HEADS * HEAD_DIM
DN_W = DN_HEADS * HEAD_DIM
MEM_W = MEM_HEADS * HEAD_DIM
DIFF_W = DIFF_HEADS * DIFF_VDIM

LANES = 128
VMEM_LIMIT = 56 * 1024 * 1024

NA_QROWS = 4
NA_KROWS = 12
NA_TQ = NA_QROWS * GRID_W
NA_TK = NA_KROWS * GRID_W


def _cparams(*sem):
    return pltpu.CompilerParams(dimension_semantics=sem, vmem_limit_bytes=VMEM_LIMIT)


def _dot(a, b):
    return jnp.dot(a, b, preferred_element_type=F32)


def _dot_nt(a, b):
    return lax.dot_general(a, b, (((1,), (1,)), ((), ())), preferred_element_type=F32)


def _silu(x):
    return x * (0.5 * jnp.tanh(0.5 * x) + 0.5)


def _rowsum_mxu(x):
    ones = jnp.ones((LANES, LANES), BF16)
    hi = x.astype(BF16)
    lo = (x - hi.astype(F32)).astype(BF16)
    return _dot(hi, ones) + _dot(lo, ones)


def _rms(x, g):
    return x * lax.rsqrt(jnp.mean(x * x, axis=-1, keepdims=True) + EPS) * g


def _proj_kernel(x_ref, g_ref, *rest, tiles, has_aux):
    w_refs = rest[:len(tiles)]
    rest = rest[len(tiles):]
    if has_aux:
        waux_ref, o_ref, aux_ref, h_ref = rest
    else:
        o_ref, h_ref = rest
    j = pl.program_id(1)

    @pl.when(j == 0)
    def _():
        h = _rms(x_ref[...], g_ref[...]).astype(BF16)
        h_ref[...] = h
        if has_aux:
            aux_ref[...] = _dot(h, waux_ref[...])

    first = 0
    for w_ref, n_tiles in zip(w_refs, tiles):
        @pl.when((j >= first) & (j < first + n_tiles))
        def _(w_ref=w_ref):
            o_ref[...] = _dot(h_ref[...], w_ref[...]).astype(o_ref.dtype)
        first += n_tiles


def _proj(x2d, g, w_groups, w_aux=None, *, tm, tn, name):
    n, d = x2d.shape
    tiles = tuple(nt for _, _, nt in w_groups)
    nc = sum(tiles) * tn
    has_aux = w_aux is not None
    in_specs = [
        pl.BlockSpec((tm, d), lambda i, j: (i, 0)),
        pl.BlockSpec((1, d), lambda i, j: (0, 0)),
    ]
    first = 0
    for _, tile0, nt in w_groups:
        in_specs.append(pl.BlockSpec(
            (d, tn), lambda i, j, first=first, tile0=tile0, nt=nt: (0, tile0 + jnp.clip(j - first, 0, nt - 1))))
        first += nt
    out_shape = [jax.ShapeDtypeStruct((n, nc), BF16)]
    out_specs = [pl.BlockSpec((tm, tn), lambda i, j: (i, j))]
    args = [x2d, g.reshape(1, d)] + [w for w, _, _ in w_groups]
    if has_aux:
        in_specs.append(pl.BlockSpec((d, LANES), lambda i, j: (0, 0)))
        out_shape.append(jax.ShapeDtypeStruct((n, LANES), F32))
        out_specs.append(pl.BlockSpec((tm, LANES), lambda i, j: (i, 0)))
        args.append(w_aux)
    res = pl.pallas_call(
        functools.partial(_proj_kernel, tiles=tiles, has_aux=has_aux),
        grid=(n // tm, nc // tn),
        in_specs=in_specs,
        out_specs=out_specs,
        out_shape=out_shape,
        scratch_shapes=[pltpu.VMEM((tm, d), BF16)],
        compiler_params=_cparams("parallel", "arbitrary"),
        name=name,
    )(*args)
    return res if has_aux else res[0]


def _outproj_kernel(*refs, n_in):
    x_ref = refs[0]
    a_refs = refs[1:1 + n_in]
    w_refs = refs[1 + n_in:1 + 2 * n_in]
    o_ref = refs[1 + 2 * n_in]
    acc = x_ref[...]
    for a_ref, w_ref in zip(a_refs, w_refs):
        acc = acc + _dot(a_ref[...], w_ref[...])
    o_ref[...] = acc


def _outproj(x2d, acts, w, *, tm, name):
    n, d = x2d.shape
    n_in = len(acts)
    in_specs = [pl.BlockSpec((tm, d), lambda i: (i, 0))]
    in_specs += [pl.BlockSpec((tm, a.shape[1]), lambda i: (i, 0)) for a in acts]
    row0 = 0
    for a in acts:
        rows = a.shape[1]
        assert row0 % rows == 0, "a weight row group must start at a multiple of its height"
        in_specs.append(pl.BlockSpec((rows, d), lambda i, blk=row0 // rows: (blk, 0)))
        row0 += rows
    weights = [w] * n_in
    return pl.pallas_call(
        functools.partial(_outproj_kernel, n_in=n_in),
        grid=(n // tm,),
        in_specs=in_specs,
        out_specs=pl.BlockSpec((tm, d), lambda i: (i, 0)),
        out_shape=jax.ShapeDtypeStruct((n, d), F32),
        compiler_params=_cparams("parallel"),
        name=name,
    )(x2d, *acts, *weights)


def _mem_kernel(q_ref, gate_ref, k_ref, v_ref, qn_ref, kn_ref, o_ref):
    scale = HEAD_DIM ** -0.5
    for h in range(MEM_HEADS):
        cols = slice(h * HEAD_DIM, (h + 1) * HEAD_DIM)
        q = (_rms(q_ref[:, cols].astype(F32), qn_ref[...]) * scale).astype(BF16)
        k = _rms(k_ref[:, cols].astype(F32), kn_ref[...]).astype(BF16)
        s = _dot_nt(q, k)
        e = jnp.exp(s - jnp.max(s, axis=-1, keepdims=True))
        p = (e * (1.0 / jnp.sum(e, axis=-1, keepdims=True))).astype(BF16)
        o = _dot(p, v_ref[:, cols])
        o_ref[:, cols] = (o * _silu(gate_ref[:, cols].astype(F32))).astype(o_ref.dtype)


def _mem_attention(proj, q_blk, gate_blk, kv, q_norm, k_norm, *, batch, tq, name):
    n = proj.shape[0]
    t = n // batch
    m = kv.shape[0] // batch
    steps = t // tq
    return pl.pallas_call(
        _mem_kernel,
        grid=(batch, steps),
        in_specs=[
            pl.BlockSpec((tq, MEM_W), lambda b, i: (b * steps + i, q_blk)),
            pl.BlockSpec((tq, MEM_W), lambda b, i: (b * steps + i, gate_blk)),
            pl.BlockSpec((m, MEM_W), lambda b, i: (b, 0)),
            pl.BlockSpec((m, MEM_W), lambda b, i: (b, 1)),
            pl.BlockSpec((1, HEAD_DIM), lambda b, i: (0, 0)),
            pl.BlockSpec((1, HEAD_DIM), lambda b, i: (0, 0)),
        ],
        out_specs=pl.BlockSpec((tq, MEM_W), lambda b, i: (b * steps + i, 0)),
        out_shape=jax.ShapeDtypeStruct((n, MEM_W), BF16),
        compiler_params=_cparams("parallel", "parallel"),
        name=name,
    )(proj, proj, kv, kv, q_norm.reshape(1, HEAD_DIM), k_norm.reshape(1, HEAD_DIM))


def _na_bias_kernel(rpb_ref, o_ref, *, n_rows):
    h = pl.program_id(0)
    n_dc = 2 * NA_WIN_W - 1
    qc = lax.broadcasted_iota(jnp.int32, (GRID_W, GRID_W), 0)
    kc = lax.broadcasted_iota(jnp.int32, (GRID_W, GRID_W), 1)
    dcm = jnp.clip(kc - qc, -(NA_WIN_W - 1), NA_WIN_W - 1) + (NA_WIN_W - 1)
    c0 = jnp.clip(qc - NA_WIN_W // 2, 0, GRID_W - NA_WIN_W)
    col_ok = (kc >= c0) & (kc < c0 + NA_WIN_W)
    neg = jnp.full((GRID_W, GRID_W), -jnp.inf, F32)
    tiles = []
    for dr in range(2 * NA_WIN_H - 1):
        t = jnp.zeros((GRID_W, GRID_W), F32)
        for dc in range(n_dc):
            t = jnp.where(dcm == dc, rpb_ref[h, dr * n_dc + dc], t)
        tiles.append(jnp.where(col_ok, t * LOG2E, neg))
    n_blocks = n_rows // NA_QROWS
    for case, j in enumerate((0, 1, n_blocks - 1)):
        start = min(max(NA_QROWS * j - NA_QROWS, 0), n_rows - NA_KROWS)
        for a in range(NA_QROWS):
            qr = NA_QROWS * j + a
            r0 = min(max(qr - NA_WIN_H // 2, 0), n_rows - NA_WIN_H)
            for b in range(NA_KROWS):
                kr = start + b
                tile = tiles[kr - qr + NA_WIN_H - 1] if r0 <= kr < r0 + NA_WIN_H else neg
                o_ref[0, case, a * GRID_W:(a + 1) * GRID_W, b * GRID_W:(b + 1) * GRID_W] = tile


def _na_bias(rpb, n_rows):
    h = rpb.shape[0]
    return pl.pallas_call(
        functools.partial(_na_bias_kernel, n_rows=n_rows),
        grid=(h,),
        in_specs=[pl.BlockSpec(memory_space=pltpu.SMEM)],
        out_specs=pl.BlockSpec((1, 3, NA_TQ, NA_TK), lambda i: (i, 0, 0, 0)),
        out_shape=jax.ShapeDtypeStruct((h, 3, NA_TQ, NA_TK), F32),
        compiler_params=_cparams("parallel"),
        name="na_bias",
    )(rpb.reshape(h, -1))


def _na_kernel(q_ref, k_ref, v_ref, gate_ref, bias_ref, qn_ref, kn_ref, o_ref, k_s, *, n_blocks):
    k_s[...] = _rms(k_ref[...].astype(F32), kn_ref[...]).astype(BF16)

    def key_start(j):
        return min(max(j - 1, 0), n_blocks - NA_KROWS // NA_QROWS) * NA_TQ

    def scores(j):
        q = _rms(q_ref[j * NA_TQ:(j + 1) * NA_TQ, :].astype(F32), qn_ref[...])
        q16 = (q * (HEAD_DIM ** -0.5 * LOG2E)).astype(BF16)
        case = 0 if j == 0 else (2 if j == n_blocks - 1 else 1)
        return _dot_nt(q16, k_s[key_start(j):key_start(j) + NA_TK, :]) + bias_ref[0, case]

    s_next = scores(0)
    for j in range(n_blocks):
        s = s_next
        if j + 1 < n_blocks:
            s_next = scores(j + 1)
        rows = slice(j * NA_TQ, (j + 1) * NA_TQ)
        e = jnp.exp2(s - jnp.max(s, axis=-1, keepdims=True))
        inv_l = 1.0 / jnp.sum(e, axis=-1, keepdims=True)
        o = _dot(e.astype(BF16), v_ref[key_start(j):key_start(j) + NA_TK, :]) * inv_l
        o_ref[rows, :] = (o * _silu(gate_ref[rows, :].astype(F32))).astype(o_ref.dtype)


def _na_attention(proj, bias, q_norm, k_norm, *, batch):
    n = proj.shape[0]
    t = n // batch
    n_blocks = t // NA_TQ
    nh = NA_HEADS
    head_block = lambda off: pl.BlockSpec((t, HEAD_DIM), lambda b, h: (b, off + h))
    return pl.pallas_call(
        functools.partial(_na_kernel, n_blocks=n_blocks),
        grid=(batch, nh),
        in_specs=[
            head_block(0),
            head_block(nh),
            head_block(2 * nh),
            head_block(3 * nh),
            pl.BlockSpec((1, 3, NA_TQ, NA_TK), lambda b, h: (h, 0, 0, 0)),
            pl.BlockSpec((1, HEAD_DIM), lambda b, h: (0, 0)),
            pl.BlockSpec((1, HEAD_DIM), lambda b, h: (0, 0)),
        ],
        out_specs=head_block(0),
        out_shape=jax.ShapeDtypeStruct((n, NA_W), BF16),
        scratch_shapes=[pltpu.VMEM((t, HEAD_DIM), BF16)],
        compiler_params=_cparams("parallel", "parallel"),
        name="na_attention",
    )(proj, proj, proj, proj, bias, q_norm.reshape(1, HEAD_DIM), k_norm.reshape(1, HEAD_DIM))


def _split3(x):
    hi = x.astype(BF16)
    r = x - hi.astype(F32)
    mid = r.astype(BF16)
    lo = (r - mid.astype(F32)).astype(BF16)
    return hi, mid, lo


def _dn_gates_kernel(ba_ref, alog_ref, dtb_ref, col_ref, row_ref, mask_s):
    tm = ba_ref.shape[0]
    ba = ba_ref[...]
    lane = lax.broadcasted_iota(jnp.int32, (1, LANES), 1)
    beta = 1.0 / (1.0 + jnp.exp(-ba))
    z = ba + dtb_ref[...]
    softplus = jnp.maximum(z, 0.0) + jnp.log(1.0 + jnp.exp(-jnp.abs(z)))
    g = -jnp.exp(alog_ref[...]) * softplus

    @pl.when((pl.program_id(0) == 0) & (pl.program_id(1) == 0))
    def _():
        r = lax.broadcasted_iota(jnp.int32, (tm, tm), 0)
        c = lax.broadcasted_iota(jnp.int32, (tm, tm), 1)
        same = (r // DN_CHUNK) == (c // DN_CHUNK)
        mask_s[0] = jnp.where(same & (c <= r), 1.0, 0.0).astype(BF16)
        mask_s[1] = jnp.where(same & (c >= r), 1.0, 0.0).astype(BF16)

    parts = _split3(g)
    m_f = mask_s[0]
    m_b = mask_s[1]
    g_f = _dot(m_f, parts[0]) + _dot(m_f, parts[1]) + _dot(m_f, parts[2])
    g_b = _dot(m_b, parts[0]) + _dot(m_b, parts[1]) + _dot(m_b, parts[2])
    nh = DN_HEADS
    out = jnp.where(lane < 2 * nh, beta, jnp.where(lane < 3 * nh, g_f, jnp.where(lane < 4 * nh, g_b, 0.0)))
    col_ref[...] = out
    for chunk in range(tm // DN_CHUNK):
        slab = out[chunk * DN_CHUNK:(chunk + 1) * DN_CHUNK, :].T
        row_ref[0, :, chunk, :] = jnp.concatenate([slab, slab], axis=1)


def _dn_gates(ba, a_log, dt_bias, *, batch, tm=512):
    n = ba.shape[0]
    t = n // batch
    steps = t // tm
    pad = jnp.zeros((2 * DN_HEADS,), F32)
    alog_row = jnp.concatenate([pad, a_log.reshape(-1).astype(F32), jnp.zeros((LANES - 4 * DN_HEADS,), F32)])
    dtb_row = jnp.concatenate([pad, dt_bias.reshape(-1).astype(F32), jnp.zeros((LANES - 4 * DN_HEADS,), F32)])
    return pl.pallas_call(
        _dn_gates_kernel,
        grid=(batch, steps),
        in_specs=[
            pl.BlockSpec((tm, LANES), lambda b, i: (b * steps + i, 0)),
            pl.BlockSpec((1, LANES), lambda b, i: (0, 0)),
            pl.BlockSpec((1, LANES), lambda b, i: (0, 0)),
        ],
        out_specs=[
            pl.BlockSpec((tm, LANES), lambda b, i: (b * steps + i, 0)),
            pl.BlockSpec((1, LANES, tm // DN_CHUNK, 2 * DN_CHUNK), lambda b, i: (b, 0, i, 0)),
        ],
        out_shape=[
            jax.ShapeDtypeStruct((n, LANES), F32),
            jax.ShapeDtypeStruct((batch, LANES, t // DN_CHUNK, 2 * DN_CHUNK), F32),
        ],
        scratch_shapes=[pltpu.VMEM((2, tm, tm), BF16)],
        compiler_params=_cparams("arbitrary", "arbitrary"),
        name="dn_gates",
    )(ba, alog_row.reshape(1, LANES), dtb_row.reshape(1, LANES))


def _dn_kernel(q_ref, k_ref, v_ref, gate_ref, cwq_ref, cwk_ref, cwv_ref, gc_ref, grf_ref, grb_ref,
               on_ref, o_ref, xp_s, qn_s, kn_s, vn_s, mq_s, n_s, of_s, ob_s):
    t = q_ref.shape[0]
    c_sz = DN_CHUNK
    n_chunks = t // c_sz
    head = pl.program_id(1)

    border = jnp.zeros((8, HEAD_DIM), F32)
    xp_s[0:8, :] = border
    xp_s[t + 8:t + 16, :] = border

    def conv_silu(x_ref, cw_ref):
        xp_s[8:t + 8, :] = x_ref[...].astype(F32)
        acc = None
        for tap in range(DN_CONV):
            start = 8 + tap - DN_CONV // 2
            term = xp_s[start:start + t, :] * cw_ref[tap:tap + 1, :]
            acc = term if acc is None else acc + term
        return _silu(acc)

    def l2n(x):
        return x * lax.rsqrt(_rowsum_mxu(x * x) + EPS)

    qn_s[...] = l2n(conv_silu(q_ref, cwq_ref)) * HEAD_DIM ** -0.5
    kn_s[...] = l2n(conv_silu(k_ref, cwk_ref))
    vn_s[...] = conv_silu(v_ref, cwv_ref)

    lane = lax.broadcasted_iota(jnp.int32, (1, LANES), 1)
    ii = lax.broadcasted_iota(jnp.int32, (c_sz, 2 * c_sz), 0)
    lane2 = lax.broadcasted_iota(jnp.int32, (c_sz, 2 * c_sz), 1)
    jj = lane2 & (c_sz - 1)
    right = lane2 >= c_sz
    eye_left = jnp.where(ii == lane2, 1.0, 0.0)
    zero_rows = jnp.zeros((c_sz, 2 * c_sz), BF16)

    n_levels = int(math.log2(c_sz))
    n_groups = n_chunks // DN_UNROLL

    def chunk_local(groups):
        chains = []
        for u in range(DN_UNROLL):
            for d, gr_ref in enumerate((grf_ref, grb_ref)):
                c = groups[d] * DN_UNROLL + u
                rows = pl.ds(pl.multiple_of(c * c_sz, c_sz), c_sz)
                kc = kn_s[rows, :]
                qc = qn_s[rows, :]
                vc = vn_s[rows, :]
                gcb = gc_ref[rows, :]
                k16 = kc.astype(BF16)
                kk = _dot_nt(k16, jnp.concatenate([k16, k16], axis=0))
                qk_raw = _dot_nt(qc.astype(BF16), k16)
                beta = jnp.sum(jnp.where(lane == d * DN_HEADS + head, gcb, 0.0), axis=1, keepdims=True)
                g_col = jnp.sum(jnp.where(lane == (2 + d) * DN_HEADS + head, gcb, 0.0), axis=1, keepdims=True)
                g_row = gr_ref[0, 0, pl.ds(c, 1), :]
                incl = (ii >= jj) if d == 0 else (ii <= jj)
                strict = (ii > jj) if d == 0 else (ii < jj)
                g_edge = g_row[:, c_sz - 1:c_sz] if d == 0 else g_row[:, 0:1]
                decay = jnp.where(incl, jnp.exp(jnp.where(incl, g_col - g_row, 0.0)), 0.0)
                e_g = jnp.exp(g_col)
                q_dec = qc * e_g
                kd = kc * jnp.exp(g_edge - g_col)
                qk = qk_raw * decay[:, :c_sz]
                lhs = jnp.concatenate([kd.T.astype(BF16), qk.astype(BF16)], axis=0)
                z = jnp.where(right & strict, -(beta * kk * decay), 0.0)
                rhs = jnp.concatenate([vc * beta, kc * (beta * e_g)], axis=1)
                chains.append([d, c, rows, None, z, lhs, q_dec, rhs])
        return chains

    n_stages = n_levels + 2

    def stage_dots(chains, stage):
        if stage < n_levels:
            return [_dot(z.astype(BF16), jnp.concatenate([zero_rows, (z + eye_left).astype(BF16)], axis=0))
                    for _, _, _, _, z, _, _, _ in chains]
        if stage == n_levels:
            return [_dot(z.astype(BF16), jnp.concatenate([rhs.astype(BF16), jnp.zeros_like(rhs, BF16)], axis=0))
                    for _, _, _, _, z, _, _, rhs in chains]
        return [_dot(lhs, sol.astype(BF16)) for _, _, _, _, sol, lhs, _, _ in chains]

    def stage_update(chains, stage, outs):
        for chain, out in zip(chains, outs):
            d, c, rows, _, z, _, q_dec, rhs = chain
            if stage < n_levels:
                chain[4] = jnp.where(right, 0.0, z) + out
            elif stage == n_levels:
                chain[4] = rhs + out
            else:
                n_s[d, c] = out[:HEAD_DIM, :HEAD_DIM]
                mq_s[d, c, 0:HEAD_DIM, :] = out[:HEAD_DIM, HEAD_DIM:].astype(BF16)
                mq_s[d, c, HEAD_DIM:HEAD_DIM + c_sz, :] = (q_dec - out[HEAD_DIM:, HEAD_DIM:]).astype(BF16)
                (of_s, ob_s)[d][rows, :] = out[HEAD_DIM:, :HEAD_DIM]

    def scan_chunks(groups, u):
        return (groups[0] * DN_UNROLL + u, groups[1] * DN_UNROLL + DN_UNROLL - 1 - u)

    def scan_dots(cs, states):
        return [_dot(mq_s[d, cs[d]], states[d].astype(BF16)) for d in range(2)]

    def scan_update(cs, states, ys):
        new_states = []
        for d, (gr_ref, o_s) in enumerate(((grf_ref, of_s), (grb_ref, ob_s))):
            rows = pl.ds(pl.multiple_of(cs[d] * c_sz, c_sz), c_sz)
            g_row = gr_ref[0, 0, pl.ds(cs[d], 1), :]
            g_edge = g_row[:, c_sz - 1:c_sz] if d == 0 else g_row[:, 0:1]
            o_s[rows, :] = o_s[rows, :] + ys[d][HEAD_DIM:]
            new_states.append(states[d] * jnp.exp(g_edge) + n_s[d, cs[d]] - ys[d][:HEAD_DIM])
        return tuple(new_states)

    def scan_group(groups, states):
        for u in range(DN_UNROLL):
            cs = scan_chunks(groups, u)
            states = scan_update(cs, states, scan_dots(cs, states))
        return states

    def fused_body(i, states):
        assert n_stages == DN_UNROLL
        scan_groups = (i - 1, n_groups - i)
        chains = chunk_local((i, n_groups - 1 - i))
        for stage in range(n_stages):
            cs = scan_chunks(scan_groups, stage)
            ys = scan_dots(cs, states)
            outs = stage_dots(chains, stage)
            states = scan_update(cs, states, ys)
            stage_update(chains, stage, outs)
        return states

    zero = jnp.zeros((HEAD_DIM, HEAD_DIM), F32)
    first = chunk_local((0, n_groups - 1))
    for stage in range(n_stages):
        stage_update(first, stage, stage_dots(first, stage))
    states = lax.fori_loop(1, n_groups, fused_body, (zero, zero))
    scan_group((n_groups - 1, 0), states)

    o = of_s[...] + ob_s[...]
    o = o * lax.rsqrt(_rowsum_mxu(o * o) * (1.0 / HEAD_DIM) + EPS) * on_ref[...]
    o_ref[...] = (o * _silu(gate_ref[...].astype(F32))).astype(o_ref.dtype)


def _deltanet(proj, conv_w, gates_col, gates_row, out_norm, *, batch, col0):
    n = proj.shape[0]
    t = n // batch
    n_chunks = t // DN_CHUNK
    nh = DN_HEADS
    gr = gates_row
    return pl.pallas_call(
        _dn_kernel,
        grid=(batch, nh),
        in_specs=[
            pl.BlockSpec((t, HEAD_DIM), lambda b, h: (b, col0 + h)),
            pl.BlockSpec((t, HEAD_DIM), lambda b, h: (b, col0 + nh + h)),
            pl.BlockSpec((t, HEAD_DIM), lambda b, h: (b, col0 + 2 * nh + h)),
            pl.BlockSpec((t, HEAD_DIM), lambda b, h: (b, col0 + 3 * nh + h)),
            pl.BlockSpec((DN_CONV, HEAD_DIM), lambda b, h: (0, h)),
            pl.BlockSpec((DN_CONV, HEAD_DIM), lambda b, h: (0, nh + h)),
            pl.BlockSpec((DN_CONV, HEAD_DIM), lambda b, h: (0, 2 * nh + h)),
            pl.BlockSpec((t, LANES), lambda b, h: (b, 0)),
            pl.BlockSpec((1, 1, n_chunks, 2 * DN_CHUNK), lambda b, h: (b, 2 * nh + h, 0, 0)),
            pl.BlockSpec((1, 1, n_chunks, 2 * DN_CHUNK), lambda b, h: (b, 3 * nh + h, 0, 0)),
            pl.BlockSpec((1, HEAD_DIM), lambda b, h: (0, 0)),
        ],
        out_specs=pl.BlockSpec((t, HEAD_DIM), lambda b, h: (b, h)),
        out_shape=jax.ShapeDtypeStruct((n, DN_W), BF16),
        scratch_shapes=[
            pltpu.VMEM((t + 16, HEAD_DIM), F32),
            pltpu.VMEM((t, HEAD_DIM), F32),
            pltpu.VMEM((t, HEAD_DIM), F32),
            pltpu.VMEM((t, HEAD_DIM), F32),
            pltpu.VMEM((2, n_chunks, HEAD_DIM + DN_CHUNK, HEAD_DIM), BF16),
            pltpu.VMEM((2, n_chunks, HEAD_DIM, HEAD_DIM), F32),
            pltpu.VMEM((t, HEAD_DIM), F32),
            pltpu.VMEM((t, HEAD_DIM), F32),
        ],
        compiler_params=_cparams("parallel", "parallel"),
        name="deltanet",
    )(proj, proj, proj, proj, conv_w, conv_w, conv_w, gates_col, gr, gr, out_norm.reshape(1, HEAD_DIM))


def _rope_tables(t):
    inv = ROPE_THETA ** (-np.arange(0, HEAD_DIM, 2, dtype=np.float64) / HEAD_DIM)
    ang = (np.arange(t, dtype=np.float32)[:, None] * inv.astype(np.float32)[None, :]).astype(np.float64)
    cos, sin = np.cos(ang), np.sin(ang)
    cos_full = np.concatenate([cos, cos], axis=1).astype(np.float32)
    sin_signed = np.concatenate([-sin, sin], axis=1).astype(np.float32)
    return jnp.asarray(cos_full), jnp.asarray(sin_signed)


def _rope(x, cos, sin_signed):
    return x * cos + pltpu.roll(x, HEAD_DIM // 2, 1) * sin_signed


def _diff_kernel(q_ref, k_ref, v_ref, gate_ref, cosq_ref, sinq_ref, cosk_ref, sink_ref, qn_ref, kn_ref,
                 lam_ref, sub_ref, o_ref, k_s, e_s, *, lam_init):
    d = HEAD_DIM
    t = k_ref.shape[0]
    tq = q_ref.shape[0]
    n_kc = t // DIFF_KC

    @pl.when(pl.program_id(2) == 0)
    def _():
        for m in range(2):
            k = _rms(k_ref[:, m * d:(m + 1) * d].astype(F32), kn_ref[...])
            k_s[m] = _rope(k, cosk_ref[...], sink_ref[...]).astype(BF16)

    lv = lam_ref[...]
    lam = (jnp.exp(jnp.sum(lv[0:1] * lv[1:2], axis=-1, keepdims=True))
           - jnp.exp(jnp.sum(lv[2:3] * lv[3:4], axis=-1, keepdims=True)) + lam_init)

    def scores(q16, m, c):
        return _dot_nt(q16, k_s[m, c * DIFF_KC:(c + 1) * DIFF_KC, :])

    q16s = []
    for m in range(2):
        q = _rms(q_ref[:, m * d:(m + 1) * d].astype(F32), qn_ref[...])
        q = _rope(q, cosq_ref[...], sinq_ref[...]) * (d ** -0.5 * LOG2E)
        q16s.append(q.astype(BF16))

    g_max = lambda ref: jnp.max(jnp.abs(ref[...]), axis=-1, keepdims=True)
    bound = g_max(qn_ref) * g_max(kn_ref) * (d * d ** -0.5 * LOG2E)

    def exact_max():
        out = []
        for m in range(2):
            mx = jnp.max(scores(q16s[m], m, 0), axis=-1, keepdims=True)
            for c in range(1, n_kc):
                mx = jnp.maximum(mx, jnp.max(scores(q16s[m], m, c), axis=-1, keepdims=True))
            out.append(mx)
        return out

    shifts = lax.cond(bound[0, 0] > DIFF_BOUND_LIMIT, exact_max,
                      lambda: [jnp.broadcast_to(bound, (tq, 1))] * 2)

    def score_pair(c):
        return [scores(q16s[m], m, c) for m in range(2)]

    lsum = [None, None]
    s_next = score_pair(0)
    for c in range(n_kc):
        s_cur = s_next
        if c + 1 < n_kc:
            s_next = score_pair(c + 1)
        for m in range(2):
            e = jnp.exp2(s_cur[m] - shifts[m])
            part = e[:, 0:LANES]
            for j in range(1, DIFF_KC // LANES):
                part = part + e[:, j * LANES:(j + 1) * LANES]
            lsum[m] = part if lsum[m] is None else lsum[m] + part
            e_s[m, :, c * DIFF_KC:(c + 1) * DIFF_KC] = e.astype(BF16)
    l0 = jnp.sum(lsum[0], axis=-1, keepdims=True)
    l1 = jnp.sum(lsum[1], axis=-1, keepdims=True)
    r16 = (lam * l0 / l1).astype(BF16)
    half = tq // 2

    def mix_pair(c):
        cols = slice(c * DIFF_KC, (c + 1) * DIFF_KC)
        return [e_s[0, r * half:(r + 1) * half, cols] - r16[r * half:(r + 1) * half] *
                e_s[1, r * half:(r + 1) * half, cols] for r in range(2)]

    accs = [None, None]
    a_next = mix_pair(0)
    for c in range(n_kc):
        a_cur = a_next
        if c + 1 < n_kc:
            a_next = mix_pair(c + 1)
        for r in range(2):
            pv = _dot(a_cur[r], v_ref[c * DIFF_KC:(c + 1) * DIFF_KC, :])
            accs[r] = pv if accs[r] is None else accs[r] + pv
    o = jnp.concatenate(accs, axis=0) * (1.0 / l0)
    o = _rms(o, sub_ref[...]) * (1.0 - lam_init)
    o_ref[...] = (o * _silu(gate_ref[...].astype(F32))).astype(o_ref.dtype)


def _diff_attention(proj, q_norm, k_norm, lam_vecs, subln, *, batch, layer_idx, tq):
    n = proj.shape[0]
    t = n // batch
    steps = t // tq
    nh = DIFF_HEADS
    w = DIFF_VDIM
    cos, sin = _rope_tables(t)
    lam_init = 0.8 - 0.6 * math.exp(-0.3 * layer_idx)
    const = lambda shape: pl.BlockSpec(shape, lambda b, h, i: (0, 0))
    return pl.pallas_call(
        functools.partial(_diff_kernel, lam_init=lam_init),
        grid=(batch, nh, steps),
        in_specs=[
            pl.BlockSpec((tq, w), lambda b, h, i: (b * steps + i, h)),
            pl.BlockSpec((t, w), lambda b, h, i: (b, nh + h)),
            pl.BlockSpec((t, w), lambda b, h, i: (b, 2 * nh + h)),
            pl.BlockSpec((tq, w), lambda b, h, i: (b * steps + i, 3 * nh + h)),
            pl.BlockSpec((tq, HEAD_DIM), lambda b, h, i: (i, 0)),
            pl.BlockSpec((tq, HEAD_DIM), lambda b, h, i: (i, 0)),
            const((t, HEAD_DIM)),
            const((t, HEAD_DIM)),
            const((1, HEAD_DIM)),
            const((1, HEAD_DIM)),
            const((4, HEAD_DIM)),
            const((1, w)),
        ],
        out_specs=pl.BlockSpec((tq, w), lambda b, h, i: (b * steps + i, h)),
        out_shape=jax.ShapeDtypeStruct((n, DIFF_W), BF16),
        scratch_shapes=[
            pltpu.VMEM((2, t, HEAD_DIM), BF16),
            pltpu.VMEM((2, tq, t), BF16),
        ],
        compiler_params=_cparams("parallel", "parallel", "arbitrary"),
        name="diff_attention",
    )(proj, proj, proj, proj, cos, sin, cos, sin, q_norm.reshape(1, HEAD_DIM), k_norm.reshape(1, HEAD_DIM),
      lam_vecs.astype(F32), subln.reshape(1, w))


def _even_layer(x2d, mem2d, mem_g, ln_g, w_in, w_mem_kv, w_out, na_q_norm, na_k_norm, na_rpb, dn_conv_w,
                dn_a_log, dn_dt_bias, dn_out_norm, mem_q_norm, mem_k_norm, *, batch):
    t = x2d.shape[0] // batch
    ba0 = 4 * NA_W + 4 * DN_W
    ba1 = ba0 + 4 * DN_HEADS
    tn = 2 * MEM_W
    w16 = w_in.astype(BF16)
    w_mem = w16[:, ba1:]
    w_ba = jnp.pad(w16[:, ba0:ba1], ((0, 0), (0, LANES - 4 * DN_HEADS)))
    proj, ba = _proj(x2d, ln_g, [(w16, 0, ba0 // tn), (w_mem, 0, 1)], w_ba, tm=1024, tn=tn, name="even_in_proj")
    kv = _proj(mem2d, mem_g, [(w_mem_kv.astype(BF16), 0, 1)], tm=mem2d.shape[0], tn=tn, name="even_mem_kv")

    bias = _na_bias(na_rpb.astype(F32), t // GRID_W)
    na_o = _na_attention(proj, bias, na_q_norm, na_k_norm, batch=batch)
    g_col, g_row = _dn_gates(ba, dn_a_log, dn_dt_bias, batch=batch)
    dn_o = _deltanet(proj, dn_conv_w.astype(F32), g_col, g_row, dn_out_norm, batch=batch,
                     col0=4 * NA_W // HEAD_DIM)
    mem_blk = ba0 // MEM_W
    mem_o = _mem_attention(proj, mem_blk, mem_blk + 1, kv, mem_q_norm, mem_k_norm, batch=batch, tq=1024,
                           name="even_mem_attention")
    return _outproj(x2d, [na_o, dn_o, mem_o], w_out.astype(BF16), tm=512, name="even_out_proj")


def _odd_layer(x2d, mem2d, mem_g, layer_idx, ln_g, w_in, w_mem_kv, w_out, q_norm, k_norm, lam_vecs, subln_g,
               mem_q_norm, mem_k_norm, *, batch):
    tn = 2 * MEM_W
    proj = _proj(x2d, ln_g, [(w_in.astype(BF16), 0, w_in.shape[1] // tn)], tm=1024, tn=tn, name="odd_in_proj")
    kv = _proj(mem2d, mem_g, [(w_mem_kv.astype(BF16), 0, 1)], tm=mem2d.shape[0], tn=tn, name="odd_mem_kv")
    diff_o = _diff_attention(proj, q_norm, k_norm, lam_vecs, subln_g, batch=batch, layer_idx=layer_idx, tq=1024)
    mem_blk = 4 * DIFF_W // MEM_W
    mem_o = _mem_attention(proj, mem_blk, mem_blk + 1, kv, mem_q_norm, mem_k_norm, batch=batch, tq=1024,
                           name="odd_mem_attention")
    return _outproj(x2d, [diff_o, mem_o], w_out.astype(BF16), tm=512, name="odd_out_proj")


def kernel(x, mem, mem_norm_g, e_ln_g, e_w_in, e_w_mem_kv, e_w_out, na_q_norm, na_k_norm, na_rpb, dn_conv_w,
           dn_a_log, dn_dt_bias, dn_out_norm, e_mem_q_norm, e_mem_k_norm, o_ln_g, o_w_in, o_w_mem_kv, o_w_out,
           df_q_norm, df_k_norm, df_lambda, df_subln, o_mem_q_norm, o_mem_k_norm):
    batch, t, d = x.shape
    depth = e_ln_g.shape[0] + o_ln_g.shape[0]
    x2d = x.reshape(batch * t, d)
    mem2d = mem.reshape(-1, d)
    for layer in range(depth):
        i = layer // 2
        if layer % 2 == 0:
            x2d = _even_layer(x2d, mem2d, mem_norm_g, e_ln_g[i], e_w_in[i], e_w_mem_kv[i], e_w_out[i],
                              na_q_norm[i], na_k_norm[i], na_rpb[i], dn_conv_w[i], dn_a_log[i], dn_dt_bias[i],
                              dn_out_norm[i], e_mem_q_norm[i], e_mem_k_norm[i], batch=batch)
        else:
            x2d = _odd_layer(x2d, mem2d, mem_norm_g, layer, o_ln_g[i], o_w_in[i], o_w_mem_kv[i], o_w_out[i],
                             df_q_norm[i], df_k_norm[i], df_lambda[i], df_subln[i], o_mem_q_norm[i],
                             o_mem_k_norm[i], batch=batch)
    return x2d.reshape(batch, t, d)
```

```python
import functools
import math

import numpy as np
import jax
import jax.numpy as jnp
from jax import lax
from jax.experimental import pallas as pl
from jax.experimental.pallas import tpu as pltpu

F32 = jnp.float32
BF16 = jnp.bfloat16

HEAD_DIM = 128
GRID_W = 64
NA_HEADS = 6
NA_WIN_H = 8
NA_WIN_W = 16
DN_HEADS = 6
DN_CONV = 5
DN_CHUNK = 64
DN_UNROLL = 8
MEM_HEADS = 4
DIFF_HEADS = 6
DIFF_VDIM = 2 * HEAD_DIM
DIFF_KC = 512
DIFF_BOUND_LIMIT = 50.0
ROPE_THETA = 10000.0
EPS = 1e-6
LOG2E = math.log2(math.e)

NA_W = NA_HEADS * HEAD_DIM
DN_W = DN_HEADS * HEAD_DIM
MEM_W = MEM_HEADS * HEAD_DIM
DIFF_W = DIFF_HEADS * DIFF_VDIM

LANES = 128
VMEM_LIMIT = 56 * 1024 * 1024

NA_QROWS = 4
NA_KROWS = 12
NA_TQ = NA_QROWS * GRID_W
NA_TK = NA_KROWS * GRID_W


def _cparams(*sem):
    return pltpu.CompilerParams(dimension_semantics=sem, vmem_limit_bytes=VMEM_LIMIT)


def _dot(a, b):
    return jnp.dot(a, b, preferred_element_type=F32)


def _dot_nt(a, b):
    return lax.dot_general(a, b, (((1,), (1,)), ((), ())), preferred_element_type=F32)


def _silu(x):
    h = 0.5 * x
    return h + h * jnp.tanh(h)


def _rowsum_mxu(x):
    ones = jnp.ones((LANES, LANES), BF16)
    hi = x.astype(BF16)
    lo = (x - hi.astype(F32)).astype(BF16)
    return _dot(hi, ones) + _dot(lo, ones)


def _rms(x, g):
    return x * lax.rsqrt(jnp.mean(x * x, axis=-1, keepdims=True) + EPS) * g


def _proj_kernel(x_ref, g_ref, *rest, tiles, has_aux):
    w_refs = rest[:len(tiles)]
    rest = rest[len(tiles):]
    if has_aux:
        waux_ref, o_ref, aux_ref, h_ref = rest
    else:
        o_ref, h_ref = rest
    j = pl.program_id(1)

    @pl.when(j == 0)
    def _():
        h = _rms(x_ref[...], g_ref[...]).astype(BF16)
        h_ref[...] = h
        if has_aux:
            aux_ref[...] = _dot(h, waux_ref[...].astype(BF16))

    first = 0
    for w_ref, n_tiles in zip(w_refs, tiles):
        @pl.when((j >= first) & (j < first + n_tiles))
        def _(w_ref=w_ref):
            o_ref[...] = _dot(h_ref[...], w_ref[...].astype(BF16)).astype(o_ref.dtype)
        first += n_tiles


def _proj(x2d, g, w_groups, w_aux=None, *, tm, tn, name):
    n, d = x2d.shape
    tiles = tuple(nt for _, _, nt in w_groups)
    nc = sum(tiles) * tn
    has_aux = w_aux is not None
    in_specs = [
        pl.BlockSpec((tm, d), lambda i, j: (i, 0)),
        pl.BlockSpec((1, d), lambda i, j: (0, 0)),
    ]
    first = 0
    weights = []
    for w, tile0, nt in w_groups:
        tile = lambda j, first=first, tile0=tile0, nt=nt: tile0 + jnp.clip(j - first, 0, nt - 1)
        mode = dict(pipeline_mode=pl.Buffered(1)) if nt == 1 else {}
        if isinstance(w, tuple):
            stack, layer = w
            in_specs.append(pl.BlockSpec((None, d, tn), lambda i, j, layer=layer, tile=tile: (layer, 0, tile(j)),
                                         **mode))
            weights.append(stack)
        else:
            in_specs.append(pl.BlockSpec((d, tn), lambda i, j, tile=tile: (0, tile(j)), **mode))
            weights.append(w)
        first += nt
    out_shape = [jax.ShapeDtypeStruct((n, nc), BF16)]
    out_specs = [pl.BlockSpec((tm, tn), lambda i, j: (i, j))]
    args = [x2d, g.reshape(1, d)] + weights
    if has_aux:
        in_specs.append(pl.BlockSpec((d, LANES), lambda i, j: (0, 0)))
        out_shape.append(jax.ShapeDtypeStruct((n, LANES), F32))
        out_specs.append(pl.BlockSpec((tm, LANES), lambda i, j: (i, 0)))
        args.append(w_aux)
    res = pl.pallas_call(
        functools.partial(_proj_kernel, tiles=tiles, has_aux=has_aux),
        grid=(n // tm, nc // tn),
        in_specs=in_specs,
        out_specs=out_specs,
        out_shape=out_shape,
        scratch_shapes=[pltpu.VMEM((tm, d), BF16)],
        compiler_params=_cparams("parallel", "arbitrary"),
        name=name,
    )(*args)
    return res if has_aux else res[0]


def _outproj_kernel(*refs, n_in):
    x_ref = refs[0]
    a_refs = refs[1:1 + n_in]
    w_refs = refs[1 + n_in:1 + 2 * n_in]
    o_ref = refs[1 + 2 * n_in]
    acc = x_ref[...]
    for a_ref, w_ref in zip(a_refs, w_refs):
        acc = acc + _dot(a_ref[...], w_ref[...])
    o_ref[...] = acc


def _outproj(x2d, acts, w, *, tm, name):
    n, d = x2d.shape
    n_in = len(acts)
    in_specs = [pl.BlockSpec((tm, d), lambda i: (i, 0))]
    in_specs += [pl.BlockSpec((tm, a.shape[1]), lambda i: (i, 0)) for a in acts]
    row0 = 0
    for a in acts:
        rows = a.shape[1]
        assert row0 % rows == 0, "a weight row group must start at a multiple of its height"
        in_specs.append(pl.BlockSpec((rows, d), lambda i, blk=row0 // rows: (blk, 0)))
        row0 += rows
    weights = [w] * n_in
    return pl.pallas_call(
        functools.partial(_outproj_kernel, n_in=n_in),
        grid=(n // tm,),
        in_specs=in_specs,
        out_specs=pl.BlockSpec((tm, d), lambda i: (i, 0)),
        out_shape=jax.ShapeDtypeStruct((n, d), F32),
        compiler_params=_cparams("parallel"),
        name=name,
    )(x2d, *acts, *weights)


def _mem_kernel(q_ref, gate_ref, k_ref, v_ref, qn_ref, kn_ref, o_ref):
    scale = HEAD_DIM ** -0.5
    for h in range(MEM_HEADS):
        cols = slice(h * HEAD_DIM, (h + 1) * HEAD_DIM)
        q = (_rms(q_ref[:, cols].astype(F32), qn_ref[...]) * scale).astype(BF16)
        k = _rms(k_ref[:, cols].astype(F32), kn_ref[...]).astype(BF16)
        s = _dot_nt(q, k)
        e = jnp.exp(s - jnp.max(s, axis=-1, keepdims=True))
        p = (e * (1.0 / jnp.sum(e, axis=-1, keepdims=True))).astype(BF16)
        o = _dot(p, v_ref[:, cols])
        o_ref[:, cols] = (o * _silu(gate_ref[:, cols].astype(F32))).astype(o_ref.dtype)


def _mem_attention(proj, q_blk, gate_blk, kv, q_norm, k_norm, *, batch, tq, name):
    n = proj.shape[0]
    t = n // batch
    m = kv.shape[0] // batch
    steps = t // tq
    return pl.pallas_call(
        _mem_kernel,
        grid=(batch, steps),
        in_specs=[
            pl.BlockSpec((tq, MEM_W), lambda b, i: (b * steps + i, q_blk)),
            pl.BlockSpec((tq, MEM_W), lambda b, i: (b * steps + i, gate_blk)),
            pl.BlockSpec((m, MEM_W), lambda b, i: (b, 0)),
            pl.BlockSpec((m, MEM_W), lambda b, i: (b, 1)),
            pl.BlockSpec((1, HEAD_DIM), lambda b, i: (0, 0)),
            pl.BlockSpec((1, HEAD_DIM), lambda b, i: (0, 0)),
        ],
        out_specs=pl.BlockSpec((tq, MEM_W), lambda b, i: (b * steps + i, 0)),
        out_shape=jax.ShapeDtypeStruct((n, MEM_W), BF16),
        compiler_params=_cparams("parallel", "parallel"),
        name=name,
    )(proj, proj, kv, kv, q_norm.reshape(1, HEAD_DIM), k_norm.reshape(1, HEAD_DIM))


def _na_bias_kernel(rpb_ref, o_ref, *, n_rows):
    h = pl.program_id(0)
    n_dc = 2 * NA_WIN_W - 1
    qc = lax.broadcasted_iota(jnp.int32, (GRID_W, GRID_W), 0)
    kc = lax.broadcasted_iota(jnp.int32, (GRID_W, GRID_W), 1)
    dcm = jnp.clip(kc - qc, -(NA_WIN_W - 1), NA_WIN_W - 1) + (NA_WIN_W - 1)
    c0 = jnp.clip(qc - NA_WIN_W // 2, 0, GRID_W - NA_WIN_W)
    col_ok = (kc >= c0) & (kc < c0 + NA_WIN_W)
    neg = jnp.full((GRID_W, GRID_W), -jnp.inf, F32)
    tiles = []
    for dr in range(2 * NA_WIN_H - 1):
        t = jnp.zeros((GRID_W, GRID_W), F32)
        for dc in range(n_dc):
            t = jnp.where(dcm == dc, rpb_ref[h, dr * n_dc + dc], t)
        tiles.append(jnp.where(col_ok, t * LOG2E, neg))
    n_blocks = n_rows // NA_QROWS
    for case, j in enumerate((0, 1, n_blocks - 1)):
        start = min(max(NA_QROWS * j - NA_QROWS, 0), n_rows - NA_KROWS)
        for a in range(NA_QROWS):
            qr = NA_QROWS * j + a
            r0 = min(max(qr - NA_WIN_H // 2, 0), n_rows - NA_WIN_H)
            for b in range(NA_KROWS):
                kr = start + b
                tile = tiles[kr - qr + NA_WIN_H - 1] if r0 <= kr < r0 + NA_WIN_H else neg
                o_ref[0, case, a * GRID_W:(a + 1) * GRID_W, b * GRID_W:(b + 1) * GRID_W] = tile


def _na_bias(rpb, n_rows):
    h = rpb.shape[0]
    return pl.pallas_call(
        functools.partial(_na_bias_kernel, n_rows=n_rows),
        grid=(h,),
        in_specs=[pl.BlockSpec(memory_space=pltpu.SMEM)],
        out_specs=pl.BlockSpec((1, 3, NA_TQ, NA_TK), lambda i: (i, 0, 0, 0)),
        out_shape=jax.ShapeDtypeStruct((h, 3, NA_TQ, NA_TK), F32),
        compiler_params=_cparams("parallel"),
        name="na_bias",
    )(rpb.reshape(h, -1))


def _na_kernel(q_ref, k_ref, v_ref, gate_ref, bias_ref, qn_ref, kn_ref, o_ref, k_s, *, n_blocks):
    k_s[...] = _rms(k_ref[...].astype(F32), kn_ref[...]).astype(BF16)

    def key_start(j):
        return min(max(j - 1, 0), n_blocks - NA_KROWS // NA_QROWS) * NA_TQ

    def scores(j):
        q = _rms(q_ref[j * NA_TQ:(j + 1) * NA_TQ, :].astype(F32), qn_ref[...])
        q16 = (q * (HEAD_DIM ** -0.5 * LOG2E)).astype(BF16)
        case = 0 if j == 0 else (2 if j == n_blocks - 1 else 1)
        return _dot_nt(q16, k_s[key_start(j):key_start(j) + NA_TK, :]) + bias_ref[0, case]

    s_next = scores(0)
    for j in range(n_blocks):
        s = s_next
        if j + 1 < n_blocks:
            s_next = scores(j + 1)
        rows = slice(j * NA_TQ, (j + 1) * NA_TQ)
        e = jnp.exp2(s - jnp.max(s, axis=-1, keepdims=True))
        inv_l = 1.0 / jnp.sum(e, axis=-1, keepdims=True)
        o = _dot(e.astype(BF16), v_ref[key_start(j):key_start(j) + NA_TK, :]) * inv_l
        o_ref[rows, :] = (o * _silu(gate_ref[rows, :].astype(F32))).astype(o_ref.dtype)


def _na_attention(proj, bias, q_norm, k_norm, *, batch):
    n = proj.shape[0]
    t = n // batch
    n_blocks = t // NA_TQ
    nh = NA_HEADS
    head_block = lambda off: pl.BlockSpec((t, HEAD_DIM), lambda b, h: (b, off + h))
    return pl.pallas_call(
        functools.partial(_na_kernel, n_blocks=n_blocks),
        grid=(batch, nh),
        in_specs=[
            head_block(0),
            head_block(nh),
            head_block(2 * nh),
            head_block(3 * nh),
            pl.BlockSpec((1, 3, NA_TQ, NA_TK), lambda b, h: (h, 0, 0, 0)),
            pl.BlockSpec((1, HEAD_DIM), lambda b, h: (0, 0)),
            pl.BlockSpec((1, HEAD_DIM), lambda b, h: (0, 0)),
        ],
        out_specs=head_block(0),
        out_shape=jax.ShapeDtypeStruct((n, NA_W), BF16),
        scratch_shapes=[pltpu.VMEM((t, HEAD_DIM), BF16)],
        compiler_params=_cparams("parallel", "parallel"),
        name="na_attention",
    )(proj, proj, proj, proj, bias, q_norm.reshape(1, HEAD_DIM), k_norm.reshape(1, HEAD_DIM))


def _split3(x):
    hi = x.astype(BF16)
    r = x - hi.astype(F32)
    mid = r.astype(BF16)
    lo = (r - mid.astype(F32)).astype(BF16)
    return hi, mid, lo


def _dn_gates_kernel(ba_ref, alog_ref, dtb_ref, col_ref, row_ref, mask_s):
    tm = ba_ref.shape[0]
    ba = ba_ref[...]
    lane = lax.broadcasted_iota(jnp.int32, (1, LANES), 1)
    beta = 1.0 / (1.0 + jnp.exp(-ba))
    z = ba + dtb_ref[...]
    softplus = jnp.maximum(z, 0.0) + jnp.log(1.0 + jnp.exp(-jnp.abs(z)))
    g = -jnp.exp(alog_ref[...]) * softplus

    @pl.when((pl.program_id(0) == 0) & (pl.program_id(1) == 0))
    def _():
        r = lax.broadcasted_iota(jnp.int32, (tm, tm), 0)
        c = lax.broadcasted_iota(jnp.int32, (tm, tm), 1)
        same = (r // DN_CHUNK) == (c // DN_CHUNK)
        mask_s[0] = jnp.where(same & (c <= r), 1.0, 0.0).astype(BF16)
        mask_s[1] = jnp.where(same & (c >= r), 1.0, 0.0).astype(BF16)

    parts = _split3(g)
    m_f = mask_s[0]
    m_b = mask_s[1]
    g_f = _dot(m_f, parts[0]) + _dot(m_f, parts[1]) + _dot(m_f, parts[2])
    g_b = _dot(m_b, parts[0]) + _dot(m_b, parts[1]) + _dot(m_b, parts[2])
    nh = DN_HEADS
    out = jnp.where(lane < 2 * nh, beta, jnp.where(lane < 3 * nh, g_f, jnp.where(lane < 4 * nh, g_b, 0.0)))
    col_ref[...] = out
    for chunk in range(tm // DN_CHUNK):
        slab = out[chunk * DN_CHUNK:(chunk + 1) * DN_CHUNK, :].T
        row_ref[0, :, chunk, :] = jnp.concatenate([slab, slab], axis=1)


def _dn_gates(ba, a_log, dt_bias, *, batch, tm=512):
    n = ba.shape[0]
    t = n // batch
    steps = t // tm
    pad = jnp.zeros((2 * DN_HEADS,), F32)
    alog_row = jnp.concatenate([pad, a_log.reshape(-1).astype(F32), jnp.zeros((LANES - 4 * DN_HEADS,), F32)])
    dtb_row = jnp.concatenate([pad, dt_bias.reshape(-1).astype(F32), jnp.zeros((LANES - 4 * DN_HEADS,), F32)])
    return pl.pallas_call(
        _dn_gates_kernel,
        grid=(batch, steps),
        in_specs=[
            pl.BlockSpec((tm, LANES), lambda b, i: (b * steps + i, 0)),
            pl.BlockSpec((1, LANES), lambda b, i: (0, 0)),
            pl.BlockSpec((1, LANES), lambda b, i: (0, 0)),
        ],
        out_specs=[
            pl.BlockSpec((tm, LANES), lambda b, i: (b * steps + i, 0)),
            pl.BlockSpec((1, LANES, tm // DN_CHUNK, 2 * DN_CHUNK), lambda b, i: (b, 0, i, 0)),
        ],
        out_shape=[
            jax.ShapeDtypeStruct((n, LANES), F32),
            jax.ShapeDtypeStruct((batch, LANES, t // DN_CHUNK, 2 * DN_CHUNK), F32),
        ],
        scratch_shapes=[pltpu.VMEM((2, tm, tm), BF16)],
        compiler_params=_cparams("arbitrary", "arbitrary"),
        name="dn_gates",
    )(ba, alog_row.reshape(1, LANES), dtb_row.reshape(1, LANES))


def _dn_kernel(q_ref, k_ref, v_ref, gate_ref, cwq_ref, cwk_ref, cwv_ref, gc_ref, grf_ref, grb_ref,
               on_ref, o_ref, xq_s, xk_s, xv_s, qn_s, kn_s, vn_s, mq_s, n_s, of_s, ob_s):
    t = q_ref.shape[0]
    c_sz = DN_CHUNK
    n_chunks = t // c_sz
    head = pl.program_id(1)

    border = jnp.zeros((8, HEAD_DIM), F32)
    for x_s, x_ref in ((xq_s, q_ref), (xk_s, k_ref), (xv_s, v_ref)):
        x_s[0:8, :] = border
        x_s[t + 8:t + 16, :] = border
        x_s[8:t + 8, :] = x_ref[...].astype(F32)

    def conv_silu(x_s, cw_ref, c):
        base = pl.multiple_of(c * c_sz, c_sz)
        acc = None
        for tap in range(DN_CONV):
            term = x_s[pl.ds(base + (8 + tap - DN_CONV // 2), c_sz), :] * cw_ref[tap:tap + 1, :]
            acc = term if acc is None else acc + term
        return _silu(acc)

    def l2n(x):
        return x * lax.rsqrt(jnp.sum(x * x, axis=-1, keepdims=True) + EPS)

    def chunk_inputs(c, rows, compute):
        if not compute:
            return qn_s[rows, :], kn_s[rows, :], vn_s[rows, :]
        q = l2n(conv_silu(xq_s, cwq_ref, c)) * HEAD_DIM ** -0.5
        k = l2n(conv_silu(xk_s, cwk_ref, c))
        v = conv_silu(xv_s, cwv_ref, c)
        qn_s[rows, :] = q
        kn_s[rows, :] = k
        vn_s[rows, :] = v
        return q, k, v

    lane = lax.broadcasted_iota(jnp.int32, (1, LANES), 1)
    ii = lax.broadcasted_iota(jnp.int32, (c_sz, 2 * c_sz), 0)
    lane2 = lax.broadcasted_iota(jnp.int32, (c_sz, 2 * c_sz), 1)
    jj = lane2 & (c_sz - 1)
    right = lane2 >= c_sz
    eye_left = jnp.where(ii == lane2, 1.0, 0.0)
    zero_rows = jnp.zeros((c_sz, 2 * c_sz), BF16)

    n_levels = int(math.log2(c_sz))
    n_groups = n_chunks // DN_UNROLL

    def chunk_local(groups, compute):
        chains = []
        for u in range(DN_UNROLL):
            for d, gr_ref in enumerate((grf_ref, grb_ref)):
                c = groups[d] * DN_UNROLL + u
                rows = pl.ds(pl.multiple_of(c * c_sz, c_sz), c_sz)
                qc, kc, vc = chunk_inputs(c, rows, compute)
                gcb = gc_ref[rows, :]
                k16 = kc.astype(BF16)
                kk = _dot_nt(k16, jnp.concatenate([k16, k16], axis=0))
                qk_raw = _dot_nt(qc.astype(BF16), k16)
                beta = jnp.sum(jnp.where(lane == d * DN_HEADS + head, gcb, 0.0), axis=1, keepdims=True)
                g_col = jnp.sum(jnp.where(lane == (2 + d) * DN_HEADS + head, gcb, 0.0), axis=1, keepdims=True)
                g_row = gr_ref[0, 0, pl.ds(c, 1), :]
                incl = (ii >= jj) if d == 0 else (ii <= jj)
                strict = (ii > jj) if d == 0 else (ii < jj)
                g_edge = g_row[:, c_sz - 1:c_sz] if d == 0 else g_row[:, 0:1]
                decay = jnp.where(incl, jnp.exp(jnp.where(incl, g_col - g_row, 0.0)), 0.0)
                e_g = jnp.exp(g_col)
                q_dec = qc * e_g
                kd = kc * jnp.exp(g_edge - g_col)
                qk = qk_raw * decay[:, :c_sz]
                lhs = jnp.concatenate([kd.T.astype(BF16), qk.astype(BF16)], axis=0)
                z = jnp.where(right & strict, -(beta * kk * decay), 0.0)
                rhs = jnp.concatenate([vc * beta, kc * (beta * e_g)], axis=1)
                chains.append([d, c, rows, None, z, lhs, q_dec, rhs])
        return chains

    n_stages = n_levels + 2

    def stage_dots(chains, stage):
        if stage < n_levels:
            return [_dot(z.astype(BF16), jnp.concatenate([zero_rows, (z + eye_left).astype(BF16)], axis=0))
                    for _, _, _, _, z, _, _, _ in chains]
        if stage == n_levels:
            return [_dot(z.astype(BF16), jnp.concatenate([rhs.astype(BF16), jnp.zeros_like(rhs, BF16)], axis=0))
                    for _, _, _, _, z, _, _, rhs in chains]
        return [_dot(lhs, sol.astype(BF16)) for _, _, _, _, sol, lhs, _, _ in chains]

    def stage_update(chains, stage, outs):
        for chain, out in zip(chains, outs):
            d, c, rows, _, z, _, q_dec, rhs = chain
            if stage < n_levels:
                chain[4] = jnp.where(right, 0.0, z) + out
            elif stage == n_levels:
                chain[4] = rhs + out
            else:
                n_s[d, c] = out[:HEAD_DIM, :HEAD_DIM]
                mq_s[d, c, 0:HEAD_DIM, :] = out[:HEAD_DIM, HEAD_DIM:].astype(BF16)
                mq_s[d, c, HEAD_DIM:HEAD_DIM + c_sz, :] = (q_dec - out[HEAD_DIM:, HEAD_DIM:]).astype(BF16)
                (of_s, ob_s)[d][rows, :] = out[HEAD_DIM:, :HEAD_DIM]

    def scan_chunks(groups, u):
        return (groups[0] * DN_UNROLL + u, groups[1] * DN_UNROLL + DN_UNROLL - 1 - u)

    def scan_dots(cs, states):
        return [_dot(mq_s[d, cs[d]], states[d].astype(BF16)) for d in range(2)]

    def scan_update(cs, states, ys):
        new_states = []
        for d, (gr_ref, o_s) in enumerate(((grf_ref, of_s), (grb_ref, ob_s))):
            rows = pl.ds(pl.multiple_of(cs[d] * c_sz, c_sz), c_sz)
            g_row = gr_ref[0, 0, pl.ds(cs[d], 1), :]
            g_edge = g_row[:, c_sz - 1:c_sz] if d == 0 else g_row[:, 0:1]
            o_s[rows, :] = o_s[rows, :] + ys[d][HEAD_DIM:]
            new_states.append(states[d] * jnp.exp(g_edge) + n_s[d, cs[d]] - ys[d][:HEAD_DIM])
        return tuple(new_states)

    def fused_body(i, states, compute):
        assert n_stages == DN_UNROLL
        scan_groups = (i - 1, n_groups - i)
        chains = chunk_local((i, n_groups - 1 - i), compute)
        for stage in range(n_stages):
            cs = scan_chunks(scan_groups, stage)
            ys = scan_dots(cs, states)
            outs = stage_dots(chains, stage)
            states = scan_update(cs, states, ys)
            stage_update(chains, stage, outs)
        return states

    assert n_groups % 2 == 0
    zero = jnp.zeros((HEAD_DIM, HEAD_DIM), F32)
    first = chunk_local((0, n_groups - 1), True)
    for stage in range(n_stages):
        stage_update(first, stage, stage_dots(first, stage))
    states = lax.fori_loop(1, n_groups // 2, functools.partial(fused_body, compute=True), (zero, zero))
    states = lax.fori_loop(n_groups // 2, n_groups, functools.partial(fused_body, compute=False), states)

    def finish(start, size):
        o = of_s[start:start + size, :] + ob_s[start:start + size, :]
        o = o * lax.rsqrt(_rowsum_mxu(o * o) * (1.0 / HEAD_DIM) + EPS) * on_ref[...]
        gate = gate_ref[start:start + size, :].astype(F32)
        o_ref[start:start + size, :] = (o * _silu(gate)).astype(o_ref.dtype)

    group_rows = DN_UNROLL * c_sz
    piece = (t - 2 * group_rows) // DN_UNROLL
    for u in range(DN_UNROLL):
        cs = scan_chunks((n_groups - 1, 0), u)
        states = scan_update(cs, states, scan_dots(cs, states))
        finish(group_rows + u * piece, piece)
    finish(0, group_rows)
    finish(t - group_rows, group_rows)


def _deltanet(proj, conv_w, gates_col, gates_row, out_norm, *, batch, col0):
    n = proj.shape[0]
    t = n // batch
    n_chunks = t // DN_CHUNK
    nh = DN_HEADS
    gr = gates_row
    return pl.pallas_call(
        _dn_kernel,
        grid=(batch, nh),
        in_specs=[
            pl.BlockSpec((t, HEAD_DIM), lambda b, h: (b, col0 + h)),
            pl.BlockSpec((t, HEAD_DIM), lambda b, h: (b, col0 + nh + h)),
            pl.BlockSpec((t, HEAD_DIM), lambda b, h: (b, col0 + 2 * nh + h)),
            pl.BlockSpec((t, HEAD_DIM), lambda b, h: (b, col0 + 3 * nh + h)),
            pl.BlockSpec((DN_CONV, HEAD_DIM), lambda b, h: (0, h)),
            pl.BlockSpec((DN_CONV, HEAD_DIM), lambda b, h: (0, nh + h)),
            pl.BlockSpec((DN_CONV, HEAD_DIM), lambda b, h: (0, 2 * nh + h)),
            pl.BlockSpec((t, LANES), lambda b, h: (b, 0)),
            pl.BlockSpec((1, 1, n_chunks, 2 * DN_CHUNK), lambda b, h: (b, 2 * nh + h, 0, 0)),
            pl.BlockSpec((1, 1, n_chunks, 2 * DN_CHUNK), lambda b, h: (b, 3 * nh + h, 0, 0)),
            pl.BlockSpec((1, HEAD_DIM), lambda b, h: (0, 0)),
        ],
        out_specs=pl.BlockSpec((t, HEAD_DIM), lambda b, h: (b, h)),
        out_shape=jax.ShapeDtypeStruct((n, DN_W), BF16),
        scratch_shapes=[
            pltpu.VMEM((t + 16, HEAD_DIM), F32),
            pltpu.VMEM((t + 16, HEAD_DIM), F32),
            pltpu.VMEM((t + 16, HEAD_DIM), F32),
            pltpu.VMEM((t, HEAD_DIM), F32),
            pltpu.VMEM((t, HEAD_DIM), F32),
            pltpu.VMEM((t, HEAD_DIM), F32),
            pltpu.VMEM((2, n_chunks, HEAD_DIM + DN_CHUNK, HEAD_DIM), BF16),
            pltpu.VMEM((2, n_chunks, HEAD_DIM, HEAD_DIM), F32),
            pltpu.VMEM((t, HEAD_DIM), F32),
            pltpu.VMEM((t, HEAD_DIM), F32),
        ],
        compiler_params=_cparams("parallel", "parallel"),
        name="deltanet",
    )(proj, proj, proj, proj, conv_w, conv_w, conv_w, gates_col, gr, gr, out_norm.reshape(1, HEAD_DIM))


def _rope_tables(t):
    inv = ROPE_THETA ** (-np.arange(0, HEAD_DIM, 2, dtype=np.float64) / HEAD_DIM)
    ang = (np.arange(t, dtype=np.float32)[:, None] * inv.astype(np.float32)[None, :]).astype(np.float64)
    cos, sin = np.cos(ang), np.sin(ang)
    cos_full = np.concatenate([cos, cos], axis=1).astype(np.float32)
    sin_signed = np.concatenate([-sin, sin], axis=1).astype(np.float32)
    return jnp.asarray(cos_full), jnp.asarray(sin_signed)


def _rope(x, cos, sin_signed):
    return x * cos + pltpu.roll(x, HEAD_DIM // 2, 1) * sin_signed


def _diff_kernel(q_ref, k_ref, v_ref, gate_ref, cosq_ref, sinq_ref, cosk_ref, sink_ref, qn_ref, kn_ref,
                 lam_ref, sub_ref, o_ref, k_s, e_s, *, lam_init):
    d = HEAD_DIM
    t = k_ref.shape[0]
    tq = q_ref.shape[0]
    n_kc = t // DIFF_KC

    @pl.when(pl.program_id(2) == 0)
    def _():
        for m in range(2):
            k = _rms(k_ref[:, m * d:(m + 1) * d].astype(F32), kn_ref[...])
            k_s[m] = _rope(k, cosk_ref[...], sink_ref[...]).astype(BF16)

    lv = lam_ref[...]
    lam = (jnp.exp(jnp.sum(lv[0:1] * lv[1:2], axis=-1, keepdims=True))
           - jnp.exp(jnp.sum(lv[2:3] * lv[3:4], axis=-1, keepdims=True)) + lam_init)

    def scores(q16, m, c):
        return _dot_nt(q16, k_s[m, c * DIFF_KC:(c + 1) * DIFF_KC, :])

    q16s = []
    for m in range(2):
        q = _rms(q_ref[:, m * d:(m + 1) * d].astype(F32), qn_ref[...])
        q = _rope(q, cosq_ref[...], sinq_ref[...]) * (d ** -0.5 * LOG2E)
        q16s.append(q.astype(BF16))

    g_max = lambda ref: jnp.max(jnp.abs(ref[...]), axis=-1, keepdims=True)
    bound = g_max(qn_ref) * g_max(kn_ref) * (d * d ** -0.5 * LOG2E)

    def exact_max():
        out = []
        for m in range(2):
            mx = jnp.max(scores(q16s[m], m, 0), axis=-1, keepdims=True)
            for c in range(1, n_kc):
                mx = jnp.maximum(mx, jnp.max(scores(q16s[m], m, c), axis=-1, keepdims=True))
            out.append(mx)
        return out

    shifts = lax.cond(bound[0, 0] > DIFF_BOUND_LIMIT, exact_max,
                      lambda: [jnp.broadcast_to(bound, (tq, 1))] * 2)

    def score_pair(c):
        return [scores(q16s[m], m, c) for m in range(2)]

    lsum = [None, None]
    s_next = score_pair(0)
    for c in range(n_kc):
        s_cur = s_next
        if c + 1 < n_kc:
            s_next = score_pair(c + 1)
        for m in range(2):
            e = jnp.exp2(s_cur[m] - shifts[m])
            part = e[:, 0:LANES]
            for j in range(1, DIFF_KC // LANES):
                part = part + e[:, j * LANES:(j + 1) * LANES]
            lsum[m] = part if lsum[m] is None else lsum[m] + part
            e_s[m, :, c * DIFF_KC:(c + 1) * DIFF_KC] = e.astype(BF16)
    l0 = jnp.sum(lsum[0], axis=-1, keepdims=True)
    l1 = jnp.sum(lsum[1], axis=-1, keepdims=True)
    r16 = (lam * l0 / l1).astype(BF16)
    half = tq // 2

    def mix_pair(c):
        cols = slice(c * DIFF_KC, (c + 1) * DIFF_KC)
        return [e_s[0, r * half:(r + 1) * half, cols] - r16[r * half:(r + 1) * half] *
                e_s[1, r * half:(r + 1) * half, cols] for r in range(2)]

    accs = [None, None]
    a_next = mix_pair(0)
    for c in range(n_kc):
        a_cur = a_next
        if c + 1 < n_kc:
            a_next = mix_pair(c + 1)
        for r in range(2):
            pv = _dot(a_cur[r], v_ref[c * DIFF_KC:(c + 1) * DIFF_KC, :])
            accs[r] = pv if accs[r] is None else accs[r] + pv
    o = jnp.concatenate(accs, axis=0) * (1.0 / l0)
    o = _rms(o, sub_ref[...]) * (1.0 - lam_init)
    o_ref[...] = (o * _silu(gate_ref[...].astype(F32))).astype(o_ref.dtype)


def _diff_attention(proj, q_norm, k_norm, lam_vecs, subln, *, batch, layer_idx, tq):
    n = proj.shape[0]
    t = n // batch
    steps = t // tq
    nh = DIFF_HEADS
    w = DIFF_VDIM
    cos, sin = _rope_tables(t)
    lam_init = 0.8 - 0.6 * math.exp(-0.3 * layer_idx)
    const = lambda shape: pl.BlockSpec(shape, lambda b, h, i: (0, 0))
    return pl.pallas_call(
        functools.partial(_diff_kernel, lam_init=lam_init),
        grid=(batch, nh, steps),
        in_specs=[
            pl.BlockSpec((tq, w), lambda b, h, i: (b * steps + i, h)),
            pl.BlockSpec((t, w), lambda b, h, i: (b, nh + h)),
            pl.BlockSpec((t, w), lambda b, h, i: (b, 2 * nh + h)),
            pl.BlockSpec((tq, w), lambda b, h, i: (b * steps + i, 3 * nh + h)),
            pl.BlockSpec((tq, HEAD_DIM), lambda b, h, i: (i, 0)),
            pl.BlockSpec((tq, HEAD_DIM), lambda b, h, i: (i, 0)),
            const((t, HEAD_DIM)),
            const((t, HEAD_DIM)),
            const((1, HEAD_DIM)),
            const((1, HEAD_DIM)),
            const((4, HEAD_DIM)),
            const((1, w)),
        ],
        out_specs=pl.BlockSpec((tq, w), lambda b, h, i: (b * steps + i, h)),
        out_shape=jax.ShapeDtypeStruct((n, DIFF_W), BF16),
        scratch_shapes=[
            pltpu.VMEM((2, t, HEAD_DIM), BF16),
            pltpu.VMEM((2, tq, t), BF16),
        ],
        compiler_params=_cparams("parallel", "parallel", "arbitrary"),
        name="diff_attention",
    )(proj, proj, proj, proj, cos, sin, cos, sin, q_norm.reshape(1, HEAD_DIM), k_norm.reshape(1, HEAD_DIM),
      lam_vecs.astype(F32), subln.reshape(1, w))


def _even_layer(x2d, mem2d, mem_g, ln_g, w_in, w_mem_kv, w_out, na_q_norm, na_k_norm, na_rpb, dn_conv_w,
                dn_a_log, dn_dt_bias, dn_out_norm, mem_q_norm, mem_k_norm, *, batch):
    t = x2d.shape[0] // batch
    ba0 = 4 * NA_W + 4 * DN_W
    ba1 = ba0 + 4 * DN_HEADS
    tn = 2 * MEM_W
    stack, li = w_in
    w_mem = stack[li, :, ba1:]
    w_ba = jnp.pad(stack[li, :, ba0:ba1], ((0, 0), (0, LANES - 4 * DN_HEADS)))
    proj, ba = _proj(x2d, ln_g, [(w_in, 0, ba0 // tn), (w_mem, 0, 1)], w_ba, tm=1024, tn=tn, name="even_in_proj")
    kv = _proj(mem2d, mem_g, [(w_mem_kv, 0, 1)], tm=mem2d.shape[0], tn=tn, name="even_mem_kv")

    bias = _na_bias(na_rpb.astype(F32), t // GRID_W)
    na_o = _na_attention(proj, bias, na_q_norm, na_k_norm, batch=batch)
    g_col, g_row = _dn_gates(ba, dn_a_log, dn_dt_bias, batch=batch)
    dn_o = _deltanet(proj, dn_conv_w.astype(F32), g_col, g_row, dn_out_norm, batch=batch,
                     col0=4 * NA_W // HEAD_DIM)
    mem_blk = ba0 // MEM_W
    mem_o = _mem_attention(proj, mem_blk, mem_blk + 1, kv, mem_q_norm, mem_k_norm, batch=batch, tq=1024,
                           name="even_mem_attention")
    return _outproj(x2d, [na_o, dn_o, mem_o], w_out.astype(BF16), tm=512, name="even_out_proj")


def _odd_layer(x2d, mem2d, mem_g, layer_idx, ln_g, w_in, w_mem_kv, w_out, q_norm, k_norm, lam_vecs, subln_g,
               mem_q_norm, mem_k_norm, *, batch):
    tn = 2 * MEM_W
    proj = _proj(x2d, ln_g, [(w_in, 0, w_in[0].shape[2] // tn)], tm=1024, tn=tn, name="odd_in_proj")
    kv = _proj(mem2d, mem_g, [(w_mem_kv, 0, 1)], tm=mem2d.shape[0], tn=tn, name="odd_mem_kv")
    diff_o = _diff_attention(proj, q_norm, k_norm, lam_vecs, subln_g, batch=batch, layer_idx=layer_idx, tq=1024)
    mem_blk = 4 * DIFF_W // MEM_W
    mem_o = _mem_attention(proj, mem_blk, mem_blk + 1, kv, mem_q_norm, mem_k_norm, batch=batch, tq=1024,
                           name="odd_mem_attention")
    return _outproj(x2d, [diff_o, mem_o], w_out.astype(BF16), tm=512, name="odd_out_proj")


def kernel(x, mem, mem_norm_g, e_ln_g, e_w_in, e_w_mem_kv, e_w_out, na_q_norm, na_k_norm, na_rpb, dn_conv_w,
           dn_a_log, dn_dt_bias, dn_out_norm, e_mem_q_norm, e_mem_k_norm, o_ln_g, o_w_in, o_w_mem_kv, o_w_out,
           df_q_norm, df_k_norm, df_lambda, df_subln, o_mem_q_norm, o_mem_k_norm):
    batch, t, d = x.shape
    depth = e_ln_g.shape[0] + o_ln_g.shape[0]
    x2d = x.reshape(batch * t, d)
    mem2d = mem.reshape(-1, d)
    for layer in range(depth):
        i = layer // 2
        if layer % 2 == 0:
            x2d = _even_layer(x2d, mem2d, mem_norm_g, e_ln_g[i], (e_w_in, i), (e_w_mem_kv, i), e_w_out[i],
                              na_q_norm[i], na_k_norm[i], na_rpb[i], dn_conv_w[i], dn_a_log[i], dn_dt_bias[i],
                              dn_out_norm[i], e_mem_q_norm[i], e_mem_k_norm[i], batch=batch)
        else:
            x2d = _odd_layer(x2d, mem2d, mem_norm_g, layer, o_ln_g[i], (o_w_in, i), (o_w_mem_kv, i), o_w_out[i],
                             df_q_norm[i], df_k_norm[i], df_lambda[i], df_subln[i], o_mem_q_norm[i],
                             o_mem_k_norm[i], batch=batch)
    return x2d.reshape(batch, t, d)
```

```python
import functools
import math

import numpy as np
import jax
import jax.numpy as jnp
from jax import lax
from jax.experimental import pallas as pl
from jax.experimental.pallas import tpu as pltpu

F32 = jnp.float32
BF16 = jnp.bfloat16

HEAD_DIM = 128
GRID_W = 64
NA_HEADS = 6
NA_WIN_H = 8
NA_WIN_W = 16
DN_HEADS = 6
DN_CONV = 5
DN_CHUNK = 64
DN_UNROLL = 8
MEM_HEADS = 4
DIFF_HEADS = 6
DIFF_VDIM = 2 * HEAD_DIM
DIFF_KC = 512
DIFF_BOUND_LIMIT = 50.0
ROPE_THETA = 10000.0
EPS = 1e-6
LOG2E = math.log2(math.e)

NA_W = NA_HEADS * HEAD_DIM
DN_W = DN_HEADS * HEAD_DIM
MEM_W = MEM_HEADS * HEAD_DIM
DIFF_W = DIFF_HEADS * DIFF_VDIM

LANES = 128
VMEM_LIMIT = 56 * 1024 * 1024

NA_QROWS = 4
NA_KROWS = 12
NA_TQ = NA_QROWS * GRID_W
NA_TK = NA_KROWS * GRID_W


def _cparams(*sem):
    return pltpu.CompilerParams(dimension_semantics=sem, vmem_limit_bytes=VMEM_LIMIT)


def _dot(a, b):
    return jnp.dot(a, b, preferred_element_type=F32)


def _dot_nt(a, b):
    return lax.dot_general(a, b, (((1,), (1,)), ((), ())), preferred_element_type=F32)


def _silu(x):
    h = 0.5 * x
    return h + h * jnp.tanh(h)


def _rowsum_mxu(x):
    ones = jnp.ones((LANES, LANES), BF16)
    hi = x.astype(BF16)
    lo = (x - hi.astype(F32)).astype(BF16)
    return _dot(hi, ones) + _dot(lo, ones)


def _rms(x, g):
    return x * lax.rsqrt(jnp.mean(x * x, axis=-1, keepdims=True) + EPS) * g


def _proj_kernel(x_ref, g_ref, *rest, tiles, has_aux):
    w_refs = rest[:len(tiles)]
    rest = rest[len(tiles):]
    if has_aux:
        waux_ref, o_ref, aux_ref, h_ref = rest
    else:
        o_ref, h_ref = rest
    j = pl.program_id(1)

    @pl.when(j == 0)
    def _():
        h = _rms(x_ref[...], g_ref[...]).astype(BF16)
        h_ref[...] = h
        if has_aux:
            aux_ref[...] = _dot(h, waux_ref[...])

    first = 0
    for w_ref, n_tiles in zip(w_refs, tiles):
        @pl.when((j >= first) & (j < first + n_tiles))
        def _(w_ref=w_ref):
            o_ref[...] = _dot(h_ref[...], w_ref[...]).astype(o_ref.dtype)
        first += n_tiles


def _proj(x2d, g, w_groups, w_aux=None, *, tm, tn, name):
    n, d = x2d.shape
    tiles = tuple(nt for _, _, nt in w_groups)
    nc = sum(tiles) * tn
    has_aux = w_aux is not None
    in_specs = [
        pl.BlockSpec((tm, d), lambda i, j: (i, 0)),
        pl.BlockSpec((1, d), lambda i, j: (0, 0)),
    ]
    first = 0
    for _, tile0, nt in w_groups:
        in_specs.append(pl.BlockSpec(
            (d, tn), lambda i, j, first=first, tile0=tile0, nt=nt: (0, tile0 + jnp.clip(j - first, 0, nt - 1))))
        first += nt
    out_shape = [jax.ShapeDtypeStruct((n, nc), BF16)]
    out_specs = [pl.BlockSpec((tm, tn), lambda i, j: (i, j))]
    args = [x2d, g.reshape(1, d)] + [w for w, _, _ in w_groups]
    if has_aux:
        in_specs.append(pl.BlockSpec((d, LANES), lambda i, j: (0, 0)))
        out_shape.append(jax.ShapeDtypeStruct((n, LANES), F32))
        out_specs.append(pl.BlockSpec((tm, LANES), lambda i, j: (i, 0)))
        args.append(w_aux)
    res = pl.pallas_call(
        functools.partial(_proj_kernel, tiles=tiles, has_aux=has_aux),
        grid=(n // tm, nc // tn),
        in_specs=in_specs,
        out_specs=out_specs,
        out_shape=out_shape,
        scratch_shapes=[pltpu.VMEM((tm, d), BF16)],
        compiler_params=_cparams("parallel", "arbitrary"),
        name=name,
    )(*args)
    return res if has_aux else res[0]


def _outproj_kernel(*refs, n_in):
    x_ref = refs[0]
    a_refs = refs[1:1 + n_in]
    w_refs = refs[1 + n_in:1 + 2 * n_in]
    o_ref = refs[1 + 2 * n_in]
    acc = x_ref[...]
    for a_ref, w_ref in zip(a_refs, w_refs):
        acc = acc + _dot(a_ref[...], w_ref[...])
    o_ref[...] = acc


def _outproj(x2d, acts, w, *, tm, name):
    n, d = x2d.shape
    n_in = len(acts)
    in_specs = [pl.BlockSpec((tm, d), lambda i: (i, 0))]
    in_specs += [pl.BlockSpec((tm, a.shape[1]), lambda i: (i, 0)) for a in acts]
    row0 = 0
    for a in acts:
        rows = a.shape[1]
        assert row0 % rows == 0, "a weight row group must start at a multiple of its height"
        in_specs.append(pl.BlockSpec((rows, d), lambda i, blk=row0 // rows: (blk, 0)))
        row0 += rows
    weights = [w] * n_in
    return pl.pallas_call(
        functools.partial(_outproj_kernel, n_in=n_in),
        grid=(n // tm,),
        in_specs=in_specs,
        out_specs=pl.BlockSpec((tm, d), lambda i: (i, 0)),
        out_shape=jax.ShapeDtypeStruct((n, d), F32),
        compiler_params=_cparams("parallel"),
        name=name,
    )(x2d, *acts, *weights)


def _mem_kernel(q_ref, gate_ref, k_ref, v_ref, qn_ref, kn_ref, o_ref):
    d = HEAD_DIM
    g_max = lambda ref: jnp.max(jnp.abs(ref[...]), axis=-1, keepdims=True)
    bound = g_max(qn_ref) * g_max(kn_ref) * (d * d ** -0.5 * LOG2E)
    use_exact = bound[0, 0] > DIFF_BOUND_LIMIT

    def head(h, exact):
        cols = slice(h * d, (h + 1) * d)
        q = (_rms(q_ref[:, cols].astype(F32), qn_ref[...]) * (d ** -0.5 * LOG2E)).astype(BF16)
        k = _rms(k_ref[:, cols].astype(F32), kn_ref[...]).astype(BF16)
        s = _dot_nt(q, k)
        shift = jnp.max(s, axis=-1, keepdims=True) if exact else bound
        e = jnp.exp2(s - shift)
        inv_l = 1.0 / jnp.sum(e, axis=-1, keepdims=True)
        o = _dot(e.astype(BF16), v_ref[:, cols]) * inv_l
        o_ref[:, cols] = (o * _silu(gate_ref[:, cols].astype(F32))).astype(o_ref.dtype)

    @pl.when(use_exact)
    def _():
        for h in range(MEM_HEADS):
            head(h, True)

    @pl.when(jnp.logical_not(use_exact))
    def _():
        for h in range(MEM_HEADS):
            head(h, False)


def _mem_attention(proj, q_blk, gate_blk, kv, q_norm, k_norm, *, batch, tq, name):
    n = proj.shape[0]
    t = n // batch
    m = kv.shape[0] // batch
    steps = t // tq
    return pl.pallas_call(
        _mem_kernel,
        grid=(batch, steps),
        in_specs=[
            pl.BlockSpec((tq, MEM_W), lambda b, i: (b * steps + i, q_blk)),
            pl.BlockSpec((tq, MEM_W), lambda b, i: (b * steps + i, gate_blk)),
            pl.BlockSpec((m, MEM_W), lambda b, i: (b, 0)),
            pl.BlockSpec((m, MEM_W), lambda b, i: (b, 1)),
            pl.BlockSpec((1, HEAD_DIM), lambda b, i: (0, 0)),
            pl.BlockSpec((1, HEAD_DIM), lambda b, i: (0, 0)),
        ],
        out_specs=pl.BlockSpec((tq, MEM_W), lambda b, i: (b * steps + i, 0)),
        out_shape=jax.ShapeDtypeStruct((n, MEM_W), BF16),
        compiler_params=_cparams("parallel", "parallel"),
        name=name,
    )(proj, proj, kv, kv, q_norm.reshape(1, HEAD_DIM), k_norm.reshape(1, HEAD_DIM))


def _na_bias_kernel(rpb_ref, o_ref, *, n_rows):
    h = pl.program_id(0)
    n_dc = 2 * NA_WIN_W - 1
    qc = lax.broadcasted_iota(jnp.int32, (GRID_W, GRID_W), 0)
    kc = lax.broadcasted_iota(jnp.int32, (GRID_W, GRID_W), 1)
    dcm = jnp.clip(kc - qc, -(NA_WIN_W - 1), NA_WIN_W - 1) + (NA_WIN_W - 1)
    c0 = jnp.clip(qc - NA_WIN_W // 2, 0, GRID_W - NA_WIN_W)
    col_ok = (kc >= c0) & (kc < c0 + NA_WIN_W)
    neg = jnp.full((GRID_W, GRID_W), -jnp.inf, F32)
    tiles = []
    for dr in range(2 * NA_WIN_H - 1):
        t = jnp.zeros((GRID_W, GRID_W), F32)
        for dc in range(n_dc):
            t = jnp.where(dcm == dc, rpb_ref[h, dr * n_dc + dc], t)
        tiles.append(jnp.where(col_ok, t * LOG2E, neg))
    n_blocks = n_rows // NA_QROWS
    for case, j in enumerate((0, 1, n_blocks - 1)):
        start = min(max(NA_QROWS * j - NA_QROWS, 0), n_rows - NA_KROWS)
        for a in range(NA_QROWS):
            qr = NA_QROWS * j + a
            r0 = min(max(qr - NA_WIN_H // 2, 0), n_rows - NA_WIN_H)
            for b in range(NA_KROWS):
                kr = start + b
                tile = tiles[kr - qr + NA_WIN_H - 1] if r0 <= kr < r0 + NA_WIN_H else neg
                o_ref[0, case, a * GRID_W:(a + 1) * GRID_W, b * GRID_W:(b + 1) * GRID_W] = tile


def _na_bias(rpb, n_rows):
    h = rpb.shape[0]
    return pl.pallas_call(
        functools.partial(_na_bias_kernel, n_rows=n_rows),
        grid=(h,),
        in_specs=[pl.BlockSpec(memory_space=pltpu.SMEM)],
        out_specs=pl.BlockSpec((1, 3, NA_TQ, NA_TK), lambda i: (i, 0, 0, 0)),
        out_shape=jax.ShapeDtypeStruct((h, 3, NA_TQ, NA_TK), F32),
        compiler_params=_cparams("parallel"),
        name="na_bias",
    )(rpb.reshape(h, -1))


def _na_kernel(q_ref, k_ref, v_ref, gate_ref, bias_ref, qn_ref, kn_ref, o_ref, k_s, *, n_blocks):
    k_s[...] = _rms(k_ref[...].astype(F32), kn_ref[...]).astype(BF16)

    def key_start(j):
        return min(max(j - 1, 0), n_blocks - NA_KROWS // NA_QROWS) * NA_TQ

    def scores(j):
        q = _rms(q_ref[j * NA_TQ:(j + 1) * NA_TQ, :].astype(F32), qn_ref[...])
        q16 = (q * (HEAD_DIM ** -0.5 * LOG2E)).astype(BF16)
        case = 0 if j == 0 else (2 if j == n_blocks - 1 else 1)
        return _dot_nt(q16, k_s[key_start(j):key_start(j) + NA_TK, :]) + bias_ref[0, case]

    s_next = scores(0)
    for j in range(n_blocks):
        s = s_next
        if j + 1 < n_blocks:
            s_next = scores(j + 1)
        rows = slice(j * NA_TQ, (j + 1) * NA_TQ)
        e = jnp.exp2(s - jnp.max(s, axis=-1, keepdims=True))
        inv_l = 1.0 / jnp.sum(e, axis=-1, keepdims=True)
        o = _dot(e.astype(BF16), v_ref[key_start(j):key_start(j) + NA_TK, :]) * inv_l
        o_ref[rows, :] = (o * _silu(gate_ref[rows, :].astype(F32))).astype(o_ref.dtype)


def _na_attention(proj, bias, q_norm, k_norm, *, batch):
    n = proj.shape[0]
    t = n // batch
    n_blocks = t // NA_TQ
    nh = NA_HEADS
    head_block = lambda off: pl.BlockSpec((t, HEAD_DIM), lambda b, h: (b, off + h))
    return pl.pallas_call(
        functools.partial(_na_kernel, n_blocks=n_blocks),
        grid=(batch, nh),
        in_specs=[
            head_block(0),
            head_block(nh),
            head_block(2 * nh),
            head_block(3 * nh),
            pl.BlockSpec((1, 3, NA_TQ, NA_TK), lambda b, h: (h, 0, 0, 0)),
            pl.BlockSpec((1, HEAD_DIM), lambda b, h: (0, 0)),
            pl.BlockSpec((1, HEAD_DIM), lambda b, h: (0, 0)),
        ],
        out_specs=head_block(0),
        out_shape=jax.ShapeDtypeStruct((n, NA_W), BF16),
        scratch_shapes=[pltpu.VMEM((t, HEAD_DIM), BF16)],
        compiler_params=_cparams("parallel", "parallel"),
        name="na_attention",
    )(proj, proj, proj, proj, bias, q_norm.reshape(1, HEAD_DIM), k_norm.reshape(1, HEAD_DIM))


def _split3(x):
    hi = x.astype(BF16)
    r = x - hi.astype(F32)
    mid = r.astype(BF16)
    lo = (r - mid.astype(F32)).astype(BF16)
    return hi, mid, lo


def _dn_gates_kernel(ba_ref, alog_ref, dtb_ref, col_ref, row_ref, mask_s):
    tm = ba_ref.shape[0]
    ba = ba_ref[...]
    lane = lax.broadcasted_iota(jnp.int32, (1, LANES), 1)
    beta = 1.0 / (1.0 + jnp.exp(-ba))
    z = ba + dtb_ref[...]
    softplus = jnp.maximum(z, 0.0) + jnp.log(1.0 + jnp.exp(-jnp.abs(z)))
    g = -jnp.exp(alog_ref[...]) * softplus

    @pl.when((pl.program_id(0) == 0) & (pl.program_id(1) == 0))
    def _():
        r = lax.broadcasted_iota(jnp.int32, (tm, tm), 0)
        c = lax.broadcasted_iota(jnp.int32, (tm, tm), 1)
        same = (r // DN_CHUNK) == (c // DN_CHUNK)
        mask_s[0] = jnp.where(same & (c <= r), 1.0, 0.0).astype(BF16)
        mask_s[1] = jnp.where(same & (c >= r), 1.0, 0.0).astype(BF16)

    parts = _split3(g)
    m_f = mask_s[0]
    m_b = mask_s[1]
    g_f = _dot(m_f, parts[0]) + _dot(m_f, parts[1]) + _dot(m_f, parts[2])
    g_b = _dot(m_b, parts[0]) + _dot(m_b, parts[1]) + _dot(m_b, parts[2])
    nh = DN_HEADS
    out = jnp.where(lane < 2 * nh, beta, jnp.where(lane < 3 * nh, g_f, jnp.where(lane < 4 * nh, g_b, 0.0)))
    col_ref[...] = out
    for chunk in range(tm // DN_CHUNK):
        slab = out[chunk * DN_CHUNK:(chunk + 1) * DN_CHUNK, :].T
        row_ref[0, :, chunk, :] = jnp.concatenate([slab, slab], axis=1)


def _dn_gates(ba, a_log, dt_bias, *, batch, tm=512):
    n = ba.shape[0]
    t = n // batch
    steps = t // tm
    pad = jnp.zeros((2 * DN_HEADS,), F32)
    alog_row = jnp.concatenate([pad, a_log.reshape(-1).astype(F32), jnp.zeros((LANES - 4 * DN_HEADS,), F32)])
    dtb_row = jnp.concatenate([pad, dt_bias.reshape(-1).astype(F32), jnp.zeros((LANES - 4 * DN_HEADS,), F32)])
    return pl.pallas_call(
        _dn_gates_kernel,
        grid=(batch, steps),
        in_specs=[
            pl.BlockSpec((tm, LANES), lambda b, i: (b * steps + i, 0)),
            pl.BlockSpec((1, LANES), lambda b, i: (0, 0)),
            pl.BlockSpec((1, LANES), lambda b, i: (0, 0)),
        ],
        out_specs=[
            pl.BlockSpec((tm, LANES), lambda b, i: (b * steps + i, 0)),
            pl.BlockSpec((1, LANES, tm // DN_CHUNK, 2 * DN_CHUNK), lambda b, i: (b, 0, i, 0)),
        ],
        out_shape=[
            jax.ShapeDtypeStruct((n, LANES), F32),
            jax.ShapeDtypeStruct((batch, LANES, t // DN_CHUNK, 2 * DN_CHUNK), F32),
        ],
        scratch_shapes=[pltpu.VMEM((2, tm, tm), BF16)],
        compiler_params=_cparams("arbitrary", "arbitrary"),
        name="dn_gates",
    )(ba, alog_row.reshape(1, LANES), dtb_row.reshape(1, LANES))


def _dn_kernel(q_ref, k_ref, v_ref, gate_ref, cwq_ref, cwk_ref, cwv_ref, gc_ref, grf_ref, grb_ref,
               on_ref, o_ref, xq_s, xk_s, xv_s, qn_s, kn_s, vn_s, mq_s, n_s, of_s, ob_s):
    t = q_ref.shape[0]
    c_sz = DN_CHUNK
    n_chunks = t // c_sz
    head = pl.program_id(1)

    border = jnp.zeros((8, HEAD_DIM), F32)
    for x_s, x_ref in ((xq_s, q_ref), (xk_s, k_ref), (xv_s, v_ref)):
        x_s[0:8, :] = border
        x_s[t + 8:t + 16, :] = border
        x_s[8:t + 8, :] = x_ref[...].astype(F32)

    def conv_silu(x_s, cw_ref, c):
        base = pl.multiple_of(c * c_sz, c_sz)
        acc = None
        for tap in range(DN_CONV):
            term = x_s[pl.ds(base + (8 + tap - DN_CONV // 2), c_sz), :] * cw_ref[tap:tap + 1, :]
            acc = term if acc is None else acc + term
        return _silu(acc)

    def l2n(x):
        return x * lax.rsqrt(jnp.sum(x * x, axis=-1, keepdims=True) + EPS)

    def chunk_inputs(c, rows, compute):
        if not compute:
            return qn_s[rows, :], kn_s[rows, :], vn_s[rows, :]
        q = l2n(conv_silu(xq_s, cwq_ref, c)) * HEAD_DIM ** -0.5
        k = l2n(conv_silu(xk_s, cwk_ref, c))
        v = conv_silu(xv_s, cwv_ref, c)
        qn_s[rows, :] = q
        kn_s[rows, :] = k
        vn_s[rows, :] = v
        return q, k, v

    lane = lax.broadcasted_iota(jnp.int32, (1, LANES), 1)
    ii = lax.broadcasted_iota(jnp.int32, (c_sz, 2 * c_sz), 0)
    lane2 = lax.broadcasted_iota(jnp.int32, (c_sz, 2 * c_sz), 1)
    jj = lane2 & (c_sz - 1)
    right = lane2 >= c_sz
    eye_left = jnp.where(ii == lane2, 1.0, 0.0)
    zero_rows = jnp.zeros((c_sz, 2 * c_sz), BF16)

    n_levels = int(math.log2(c_sz))
    n_groups = n_chunks // DN_UNROLL

    def chunk_local(groups, compute):
        chains = []
        for u in range(DN_UNROLL):
            for d, gr_ref in enumerate((grf_ref, grb_ref)):
                c = groups[d] * DN_UNROLL + u
                rows = pl.ds(pl.multiple_of(c * c_sz, c_sz), c_sz)
                qc, kc, vc = chunk_inputs(c, rows, compute)
                gcb = gc_ref[rows, :]
                k16 = kc.astype(BF16)
                kk = _dot_nt(k16, jnp.concatenate([k16, k16], axis=0))
                qk_raw = _dot_nt(qc.astype(BF16), k16)
                beta = jnp.sum(jnp.where(lane == d * DN_HEADS + head, gcb, 0.0), axis=1, keepdims=True)
                g_col = jnp.sum(jnp.where(lane == (2 + d) * DN_HEADS + head, gcb, 0.0), axis=1, keepdims=True)
                g_row = gr_ref[0, 0, pl.ds(c, 1), :]
                incl = (ii >= jj) if d == 0 else (ii <= jj)
                strict = (ii > jj) if d == 0 else (ii < jj)
                g_edge = g_row[:, c_sz - 1:c_sz] if d == 0 else g_row[:, 0:1]
                decay = jnp.where(incl, jnp.exp(jnp.where(incl, g_col - g_row, 0.0)), 0.0)
                e_g = jnp.exp(g_col)
                q_dec = qc * e_g
                kd = kc * jnp.exp(g_edge - g_col)
                qk = qk_raw * decay[:, :c_sz]
                lhs = jnp.concatenate([kd.T.astype(BF16), qk.astype(BF16)], axis=0)
                z = jnp.where(right & strict, -(beta * kk * decay), 0.0)
                rhs = jnp.concatenate([vc * beta, kc * (beta * e_g)], axis=1)
                chains.append([d, c, rows, None, z, lhs, q_dec, rhs])
        return chains

    n_stages = n_levels + 2

    def stage_dots(chains, stage):
        if stage < n_levels:
            return [_dot(z.astype(BF16), jnp.concatenate([zero_rows, (z + eye_left).astype(BF16)], axis=0))
                    for _, _, _, _, z, _, _, _ in chains]
        if stage == n_levels:
            return [_dot(z.astype(BF16), jnp.concatenate([rhs.astype(BF16), jnp.zeros_like(rhs, BF16)], axis=0))
                    for _, _, _, _, z, _, _, rhs in chains]
        return [_dot(lhs, sol.astype(BF16)) for _, _, _, _, sol, lhs, _, _ in chains]

    def stage_update(chains, stage, outs):
        for chain, out in zip(chains, outs):
            d, c, rows, _, z, _, q_dec, rhs = chain
            if stage < n_levels:
                chain[4] = jnp.where(right, 0.0, z) + out
            elif stage == n_levels:
                chain[4] = rhs + out
            else:
                n_s[d, c] = out[:HEAD_DIM, :HEAD_DIM]
                mq_s[d, c, 0:HEAD_DIM, :] = out[:HEAD_DIM, HEAD_DIM:].astype(BF16)
                mq_s[d, c, HEAD_DIM:HEAD_DIM + c_sz, :] = (q_dec - out[HEAD_DIM:, HEAD_DIM:]).astype(BF16)
                (of_s, ob_s)[d][rows, :] = out[HEAD_DIM:, :HEAD_DIM]

    def scan_chunks(groups, u):
        return (groups[0] * DN_UNROLL + u, groups[1] * DN_UNROLL + DN_UNROLL - 1 - u)

    def scan_dots(cs, states):
        return [_dot(mq_s[d, cs[d]], states[d].astype(BF16)) for d in range(2)]

    def scan_update(cs, states, ys):
        new_states = []
        for d, (gr_ref, o_s) in enumerate(((grf_ref, of_s), (grb_ref, ob_s))):
            rows = pl.ds(pl.multiple_of(cs[d] * c_sz, c_sz), c_sz)
            g_row = gr_ref[0, 0, pl.ds(cs[d], 1), :]
            g_edge = g_row[:, c_sz - 1:c_sz] if d == 0 else g_row[:, 0:1]
            o_s[rows, :] = o_s[rows, :] + ys[d][HEAD_DIM:]
            new_states.append(states[d] * jnp.exp(g_edge) + n_s[d, cs[d]] - ys[d][:HEAD_DIM])
        return tuple(new_states)

    def fused_body(i, states, compute):
        assert n_stages == DN_UNROLL
        scan_groups = (i - 1, n_groups - i)
        chains = chunk_local((i, n_groups - 1 - i), compute)
        for stage in range(n_stages):
            cs = scan_chunks(scan_groups, stage)
            ys = scan_dots(cs, states)
            outs = stage_dots(chains, stage)
            states = scan_update(cs, states, ys)
            stage_update(chains, stage, outs)
        return states

    assert n_groups % 2 == 0
    zero = jnp.zeros((HEAD_DIM, HEAD_DIM), F32)
    first = chunk_local((0, n_groups - 1), True)
    for stage in range(n_stages):
        stage_update(first, stage, stage_dots(first, stage))
    states = lax.fori_loop(1, n_groups // 2, functools.partial(fused_body, compute=True), (zero, zero))
    states = lax.fori_loop(n_groups // 2, n_groups, functools.partial(fused_body, compute=False), states)

    def finish(start, size):
        o = of_s[start:start + size, :] + ob_s[start:start + size, :]
        o = o * lax.rsqrt(_rowsum_mxu(o * o) * (1.0 / HEAD_DIM) + EPS) * on_ref[...]
        gate = gate_ref[start:start + size, :].astype(F32)
        o_ref[start:start + size, :] = (o * _silu(gate)).astype(o_ref.dtype)

    group_rows = DN_UNROLL * c_sz
    piece = (t - 2 * group_rows) // DN_UNROLL
    for u in range(DN_UNROLL):
        cs = scan_chunks((n_groups - 1, 0), u)
        states = scan_update(cs, states, scan_dots(cs, states))
        finish(group_rows + u * piece, piece)
    finish(0, group_rows)
    finish(t - group_rows, group_rows)


def _deltanet(proj, conv_w, gates_col, gates_row, out_norm, *, batch, col0):
    n = proj.shape[0]
    t = n // batch
    n_chunks = t // DN_CHUNK
    nh = DN_HEADS
    gr = gates_row
    return pl.pallas_call(
        _dn_kernel,
        grid=(batch, nh),
        in_specs=[
            pl.BlockSpec((t, HEAD_DIM), lambda b, h: (b, col0 + h)),
            pl.BlockSpec((t, HEAD_DIM), lambda b, h: (b, col0 + nh + h)),
            pl.BlockSpec((t, HEAD_DIM), lambda b, h: (b, col0 + 2 * nh + h)),
            pl.BlockSpec((t, HEAD_DIM), lambda b, h: (b, col0 + 3 * nh + h)),
            pl.BlockSpec((DN_CONV, HEAD_DIM), lambda b, h: (0, h)),
            pl.BlockSpec((DN_CONV, HEAD_DIM), lambda b, h: (0, nh + h)),
            pl.BlockSpec((DN_CONV, HEAD_DIM), lambda b, h: (0, 2 * nh + h)),
            pl.BlockSpec((t, LANES), lambda b, h: (b, 0)),
            pl.BlockSpec((1, 1, n_chunks, 2 * DN_CHUNK), lambda b, h: (b, 2 * nh + h, 0, 0)),
            pl.BlockSpec((1, 1, n_chunks, 2 * DN_CHUNK), lambda b, h: (b, 3 * nh + h, 0, 0)),
            pl.BlockSpec((1, HEAD_DIM), lambda b, h: (0, 0)),
        ],
        out_specs=pl.BlockSpec((t, HEAD_DIM), lambda b, h: (b, h)),
        out_shape=jax.ShapeDtypeStruct((n, DN_W), BF16),
        scratch_shapes=[
            pltpu.VMEM((t + 16, HEAD_DIM), F32),
            pltpu.VMEM((t + 16, HEAD_DIM), F32),
            pltpu.VMEM((t + 16, HEAD_DIM), F32),
            pltpu.VMEM((t, HEAD_DIM), F32),
            pltpu.VMEM((t, HEAD_DIM), F32),
            pltpu.VMEM((t, HEAD_DIM), F32),
            pltpu.VMEM((2, n_chunks, HEAD_DIM + DN_CHUNK, HEAD_DIM), BF16),
            pltpu.VMEM((2, n_chunks, HEAD_DIM, HEAD_DIM), F32),
            pltpu.VMEM((t, HEAD_DIM), F32),
            pltpu.VMEM((t, HEAD_DIM), F32),
        ],
        compiler_params=_cparams("parallel", "parallel"),
        name="deltanet",
    )(proj, proj, proj, proj, conv_w, conv_w, conv_w, gates_col, gr, gr, out_norm.reshape(1, HEAD_DIM))


def _rope_tables(t):
    inv = ROPE_THETA ** (-np.arange(0, HEAD_DIM, 2, dtype=np.float64) / HEAD_DIM)
    ang = (np.arange(t, dtype=np.float32)[:, None] * inv.astype(np.float32)[None, :]).astype(np.float64)
    cos, sin = np.cos(ang), np.sin(ang)
    cos_full = np.concatenate([cos, cos], axis=1).astype(np.float32)
    sin_signed = np.concatenate([-sin, sin], axis=1).astype(np.float32)
    return jnp.asarray(cos_full), jnp.asarray(sin_signed)


def _rope(x, cos, sin_signed):
    return x * cos + pltpu.roll(x, HEAD_DIM // 2, 1) * sin_signed


def _diff_kernel(q_ref, k_ref, v_ref, gate_ref, cosq_ref, sinq_ref, cosk_ref, sink_ref, qn_ref, kn_ref,
                 lam_ref, sub_ref, o_ref, k_s, e_s, *, lam_init):
    d = HEAD_DIM
    t = k_ref.shape[0]
    tq = q_ref.shape[0]
    n_kc = t // DIFF_KC

    @pl.when(pl.program_id(2) == 0)
    def _():
        for m in range(2):
            k = _rms(k_ref[:, m * d:(m + 1) * d].astype(F32), kn_ref[...])
            k_s[m] = _rope(k, cosk_ref[...], sink_ref[...]).astype(BF16)

    lv = lam_ref[...]
    lam = (jnp.exp(jnp.sum(lv[0:1] * lv[1:2], axis=-1, keepdims=True))
           - jnp.exp(jnp.sum(lv[2:3] * lv[3:4], axis=-1, keepdims=True)) + lam_init)

    def scores(q16, m, c):
        return _dot_nt(q16, k_s[m, c * DIFF_KC:(c + 1) * DIFF_KC, :])

    q16s = []
    for m in range(2):
        q = _rms(q_ref[:, m * d:(m + 1) * d].astype(F32), qn_ref[...])
        q = _rope(q, cosq_ref[...], sinq_ref[...]) * (d ** -0.5 * LOG2E)
        q16s.append(q.astype(BF16))

    g_max = lambda ref: jnp.max(jnp.abs(ref[...]), axis=-1, keepdims=True)
    bound = g_max(qn_ref) * g_max(kn_ref) * (d * d ** -0.5 * LOG2E)

    def exact_max():
        out = []
        for m in range(2):
            mx = jnp.max(scores(q16s[m], m, 0), axis=-1, keepdims=True)
            for c in range(1, n_kc):
                mx = jnp.maximum(mx, jnp.max(scores(q16s[m], m, c), axis=-1, keepdims=True))
            out.append(mx)
        return out

    shifts = lax.cond(bound[0, 0] > DIFF_BOUND_LIMIT, exact_max,
                      lambda: [jnp.broadcast_to(bound, (tq, 1))] * 2)

    def score_pair(c):
        return [scores(q16s[m], m, c) for m in range(2)]

    lsum = [None, None]
    s_next = score_pair(0)
    for c in range(n_kc):
        s_cur = s_next
        if c + 1 < n_kc:
            s_next = score_pair(c + 1)
        for m in range(2):
            e = jnp.exp2(s_cur[m] - shifts[m])
            part = e[:, 0:LANES]
            for j in range(1, DIFF_KC // LANES):
                part = part + e[:, j * LANES:(j + 1) * LANES]
            lsum[m] = part if lsum[m] is None else lsum[m] + part
            e_s[m, :, c * DIFF_KC:(c + 1) * DIFF_KC] = e.astype(BF16)
    l0 = jnp.sum(lsum[0], axis=-1, keepdims=True)
    l1 = jnp.sum(lsum[1], axis=-1, keepdims=True)
    r16 = (lam * l0 / l1).astype(BF16)
    half = tq // 2

    def mix_pair(c):
        cols = slice(c * DIFF_KC, (c + 1) * DIFF_KC)
        return [e_s[0, r * half:(r + 1) * half, cols] - r16[r * half:(r + 1) * half] *
                e_s[1, r * half:(r + 1) * half, cols] for r in range(2)]

    accs = [None, None]
    a_next = mix_pair(0)
    for c in range(n_kc):
        a_cur = a_next
        if c + 1 < n_kc:
            a_next = mix_pair(c + 1)
        for r in range(2):
            pv = _dot(a_cur[r], v_ref[c * DIFF_KC:(c + 1) * DIFF_KC, :])
            accs[r] = pv if accs[r] is None else accs[r] + pv
    o = jnp.concatenate(accs, axis=0) * (1.0 / l0)
    o = _rms(o, sub_ref[...]) * (1.0 - lam_init)
    o_ref[...] = (o * _silu(gate_ref[...].astype(F32))).astype(o_ref.dtype)


def _diff_attention(proj, q_norm, k_norm, lam_vecs, subln, *, batch, layer_idx, tq):
    n = proj.shape[0]
    t = n // batch
    steps = t // tq
    nh = DIFF_HEADS
    w = DIFF_VDIM
    cos, sin = _rope_tables(t)
    lam_init = 0.8 - 0.6 * math.exp(-0.3 * layer_idx)
    const = lambda shape: pl.BlockSpec(shape, lambda b, h, i: (0, 0))
    return pl.pallas_call(
        functools.partial(_diff_kernel, lam_init=lam_init),
        grid=(batch, nh, steps),
        in_specs=[
            pl.BlockSpec((tq, w), lambda b, h, i: (b * steps + i, h)),
            pl.BlockSpec((t, w), lambda b, h, i: (b, nh + h)),
            pl.BlockSpec((t, w), lambda b, h, i: (b, 2 * nh + h)),
            pl.BlockSpec((tq, w), lambda b, h, i: (b * steps + i, 3 * nh + h)),
            pl.BlockSpec((tq, HEAD_DIM), lambda b, h, i: (i, 0)),
            pl.BlockSpec((tq, HEAD_DIM), lambda b, h, i: (i, 0)),
            const((t, HEAD_DIM)),
            const((t, HEAD_DIM)),
            const((1, HEAD_DIM)),
            const((1, HEAD_DIM)),
            const((4, HEAD_DIM)),
            const((1, w)),
        ],
        out_specs=pl.BlockSpec((tq, w), lambda b, h, i: (b * steps + i, h)),
        out_shape=jax.ShapeDtypeStruct((n, DIFF_W), BF16),
        scratch_shapes=[
            pltpu.VMEM((2, t, HEAD_DIM), BF16),
            pltpu.VMEM((2, tq, t), BF16),
        ],
        compiler_params=_cparams("parallel", "parallel", "arbitrary"),
        name="diff_attention",
    )(proj, proj, proj, proj, cos, sin, cos, sin, q_norm.reshape(1, HEAD_DIM), k_norm.reshape(1, HEAD_DIM),
      lam_vecs.astype(F32), subln.reshape(1, w))


def _even_layer(x2d, mem2d, mem_g, ln_g, w_in, w_mem_kv, w_out, na_q_norm, na_k_norm, na_rpb, dn_conv_w,
                dn_a_log, dn_dt_bias, dn_out_norm, mem_q_norm, mem_k_norm, *, batch):
    t = x2d.shape[0] // batch
    ba0 = 4 * NA_W + 4 * DN_W
    ba1 = ba0 + 4 * DN_HEADS
    tn = 2 * MEM_W
    w16 = w_in.astype(BF16)
    w_mem = w16[:, ba1:]
    w_ba = jnp.pad(w16[:, ba0:ba1], ((0, 0), (0, LANES - 4 * DN_HEADS)))
    proj, ba = _proj(x2d, ln_g, [(w16, 0, ba0 // tn), (w_mem, 0, 1)], w_ba, tm=1024, tn=tn, name="even_in_proj")
    kv = _proj(mem2d, mem_g, [(w_mem_kv.astype(BF16), 0, 1)], tm=mem2d.shape[0], tn=tn, name="even_mem_kv")

    bias = _na_bias(na_rpb.astype(F32), t // GRID_W)
    na_o = _na_attention(proj, bias, na_q_norm, na_k_norm, batch=batch)
    g_col, g_row = _dn_gates(ba, dn_a_log, dn_dt_bias, batch=batch)
    dn_o = _deltanet(proj, dn_conv_w.astype(F32), g_col, g_row, dn_out_norm, batch=batch,
                     col0=4 * NA_W // HEAD_DIM)
    mem_blk = ba0 // MEM_W
    mem_o = _mem_attention(proj, mem_blk, mem_blk + 1, kv, mem_q_norm, mem_k_norm, batch=batch, tq=1024,
                           name="even_mem_attention")
    return _outproj(x2d, [na_o, dn_o, mem_o], w_out.astype(BF16), tm=512, name="even_out_proj")


def _odd_layer(x2d, mem2d, mem_g, layer_idx, ln_g, w_in, w_mem_kv, w_out, q_norm, k_norm, lam_vecs, subln_g,
               mem_q_norm, mem_k_norm, *, batch):
    tn = 2 * MEM_W
    proj = _proj(x2d, ln_g, [(w_in.astype(BF16), 0, w_in.shape[1] // tn)], tm=1024, tn=tn, name="odd_in_proj")
    kv = _proj(mem2d, mem_g, [(w_mem_kv.astype(BF16), 0, 1)], tm=mem2d.shape[0], tn=tn, name="odd_mem_kv")
    diff_o = _diff_attention(proj, q_norm, k_norm, lam_vecs, subln_g, batch=batch, layer_idx=layer_idx, tq=1024)
    mem_blk = 4 * DIFF_W // MEM_W
    mem_o = _mem_attention(proj, mem_blk, mem_blk + 1, kv, mem_q_norm, mem_k_norm, batch=batch, tq=1024,
                           name="odd_mem_attention")
    return _outproj(x2d, [diff_o, mem_o], w_out.astype(BF16), tm=512, name="odd_out_proj")


def kernel(x, mem, mem_norm_g, e_ln_g, e_w_in, e_w_mem_kv, e_w_out, na_q_norm, na_k_norm, na_rpb, dn_conv_w,
           dn_a_log, dn_dt_bias, dn_out_norm, e_mem_q_norm, e_mem_k_norm, o_ln_g, o_w_in, o_w_mem_kv, o_w_out,
           df_q_norm, df_k_norm, df_lambda, df_subln, o_mem_q_norm, o_mem_k_norm):
    batch, t, d = x.shape
    depth = e_ln_g.shape[0] + o_ln_g.shape[0]
    x2d = x.reshape(batch * t, d)
    mem2d = mem.reshape(-1, d)
    for layer in range(depth):
        i = layer // 2
        if layer % 2 == 0:
            x2d = _even_layer(x2d, mem2d, mem_norm_g, e_ln_g[i], e_w_in[i], e_w_mem_kv[i], e_w_out[i],
                              na_q_norm[i], na_k_norm[i], na_rpb[i], dn_conv_w[i], dn_a_log[i], dn_dt_bias[i],
                              dn_out_norm[i], e_mem_q_norm[i], e_mem_k_norm[i], batch=batch)
        else:
            x2d = _odd_layer(x2d, mem2d, mem_norm_g, layer, o_ln_g[i], o_w_in[i], o_w_mem_kv[i], o_w_out[i],
                             df_q_norm[i], df_k_norm[i], df_lambda[i], df_subln[i], o_mem_q_norm[i],
                             o_mem_k_norm[i], batch=batch)
    return x2d.reshape(batch, t, d)
```

```python
import functools
import math

import numpy as np
import jax
import jax.numpy as jnp
from jax import lax
from jax.experimental import pallas as pl
from jax.experimental.pallas import tpu as pltpu

F32 = jnp.float32
BF16 = jnp.bfloat16

HEAD_DIM = 128
GRID_W = 64
NA_HEADS = 6
NA_WIN_H = 8
NA_WIN_W = 16
DN_HEADS = 6
DN_CONV = 5
DN_CHUNK = 64
DN_UNROLL = 8
MEM_HEADS = 4
DIFF_HEADS = 6
DIFF_VDIM = 2 * HEAD_DIM
DIFF_KC = 512
DIFF_BOUND_LIMIT = 50.0
ROPE_THETA = 10000.0
EPS = 1e-6
LOG2E = math.log2(math.e)

NA_W = NA_HEADS * HEAD_DIM
DN_W = DN_HEADS * HEAD_DIM
MEM_W = MEM_HEADS * HEAD_DIM
DIFF_W = DIFF_HEADS * DIFF_VDIM

LANES = 128
VMEM_LIMIT = 56 * 1024 * 1024

NA_QROWS = 4
NA_KROWS = 12
NA_TQ = NA_QROWS * GRID_W
NA_TK = NA_KROWS * GRID_W


def _cparams(*sem):
    return pltpu.CompilerParams(dimension_semantics=sem, vmem_limit_bytes=VMEM_LIMIT)


def _dot(a, b):
    return jnp.dot(a, b, preferred_element_type=F32)


def _dot_nt(a, b):
    return lax.dot_general(a, b, (((1,), (1,)), ((), ())), preferred_element_type=F32)


def _silu(x):
    h = 0.5 * x
    return h + h * jnp.tanh(h)


def _rowsum_mxu(x):
    ones = jnp.ones((LANES, LANES), BF16)
    hi = x.astype(BF16)
    lo = (x - hi.astype(F32)).astype(BF16)
    return _dot(hi, ones) + _dot(lo, ones)


def _rms(x, g):
    return x * lax.rsqrt(jnp.mean(x * x, axis=-1, keepdims=True) + EPS) * g


def _proj_kernel(x_ref, g_ref, *rest, tiles, has_aux):
    w_refs = rest[:len(tiles)]
    rest = rest[len(tiles):]
    if has_aux:
        waux_ref, o_ref, aux_ref, h_ref = rest
    else:
        o_ref, h_ref = rest
    j = pl.program_id(1)

    @pl.when(j == 0)
    def _():
        h = _rms(x_ref[...], g_ref[...]).astype(BF16)
        h_ref[...] = h
        if has_aux:
            aux_ref[...] = _dot(h, waux_ref[...])

    first = 0
    for w_ref, n_tiles in zip(w_refs, tiles):
        @pl.when((j >= first) & (j < first + n_tiles))
        def _(w_ref=w_ref):
            o_ref[...] = _dot(h_ref[...], w_ref[...]).astype(o_ref.dtype)
        first += n_tiles


def _proj(x2d, g, w_groups, w_aux=None, *, tm, tn, name):
    n, d = x2d.shape
    tiles = tuple(nt for _, _, nt in w_groups)
    nc = sum(tiles) * tn
    has_aux = w_aux is not None
    in_specs = [
        pl.BlockSpec((tm, d), lambda i, j: (i, 0)),
        pl.BlockSpec((1, d), lambda i, j: (0, 0)),
    ]
    first = 0
    for _, tile0, nt in w_groups:
        in_specs.append(pl.BlockSpec(
            (d, tn), lambda i, j, first=first, tile0=tile0, nt=nt: (0, tile0 + jnp.clip(j - first, 0, nt - 1))))
        first += nt
    out_shape = [jax.ShapeDtypeStruct((n, nc), BF16)]
    out_specs = [pl.BlockSpec((tm, tn), lambda i, j: (i, j))]
    args = [x2d, g.reshape(1, d)] + [w for w, _, _ in w_groups]
    if has_aux:
        in_specs.append(pl.BlockSpec((d, LANES), lambda i, j: (0, 0)))
        out_shape.append(jax.ShapeDtypeStruct((n, LANES), F32))
        out_specs.append(pl.BlockSpec((tm, LANES), lambda i, j: (i, 0)))
        args.append(w_aux)
    res = pl.pallas_call(
        functools.partial(_proj_kernel, tiles=tiles, has_aux=has_aux),
        grid=(n // tm, nc // tn),
        in_specs=in_specs,
        out_specs=out_specs,
        out_shape=out_shape,
        scratch_shapes=[pltpu.VMEM((tm, d), BF16)],
        compiler_params=_cparams("parallel", "arbitrary"),
        name=name,
    )(*args)
    return res if has_aux else res[0]


def _outproj_kernel(*refs, n_in):
    x_ref = refs[0]
    a_refs = refs[1:1 + n_in]
    w_refs = refs[1 + n_in:1 + 2 * n_in]
    o_ref = refs[1 + 2 * n_in]
    acc = x_ref[...]
    for a_ref, w_ref in zip(a_refs, w_refs):
        acc = acc + _dot(a_ref[...], w_ref[...])
    o_ref[...] = acc


def _outproj(x2d, acts, w, *, tm, name):
    n, d = x2d.shape
    n_in = len(acts)
    in_specs = [pl.BlockSpec((tm, d), lambda i: (i, 0))]
    in_specs += [pl.BlockSpec((tm, a.shape[1]), lambda i: (i, 0)) for a in acts]
    row0 = 0
    for a in acts:
        rows = a.shape[1]
        assert row0 % rows == 0, "a weight row group must start at a multiple of its height"
        in_specs.append(pl.BlockSpec((rows, d), lambda i, blk=row0 // rows: (blk, 0)))
        row0 += rows
    weights = [w] * n_in
    return pl.pallas_call(
        functools.partial(_outproj_kernel, n_in=n_in),
        grid=(n // tm,),
        in_specs=in_specs,
        out_specs=pl.BlockSpec((tm, d), lambda i: (i, 0)),
        out_shape=jax.ShapeDtypeStruct((n, d), F32),
        compiler_params=_cparams("parallel"),
        name=name,
    )(x2d, *acts, *weights)


def _mem_kernel(q_ref, gate_ref, k_ref, v_ref, qn_ref, kn_ref, o_ref):
    d = HEAD_DIM
    g_max = lambda ref: jnp.max(jnp.abs(ref[...]), axis=-1, keepdims=True)
    bound = g_max(qn_ref) * g_max(kn_ref) * (d * d ** -0.5 * LOG2E)
    use_exact = bound[0, 0] > DIFF_BOUND_LIMIT

    def head(h, exact):
        cols = slice(h * d, (h + 1) * d)
        q = (_rms(q_ref[:, cols].astype(F32), qn_ref[...]) * (d ** -0.5 * LOG2E)).astype(BF16)
        k = _rms(k_ref[:, cols].astype(F32), kn_ref[...]).astype(BF16)
        s = _dot_nt(q, k)
        shift = jnp.max(s, axis=-1, keepdims=True) if exact else bound
        e = jnp.exp2(s - shift)
        inv_l = 1.0 / jnp.sum(e, axis=-1, keepdims=True)
        o = _dot(e.astype(BF16), v_ref[:, cols]) * inv_l
        o_ref[:, cols] = (o * _silu(gate_ref[:, cols].astype(F32))).astype(o_ref.dtype)

    @pl.when(use_exact)
    def _():
        for h in range(MEM_HEADS):
            head(h, True)

    @pl.when(jnp.logical_not(use_exact))
    def _():
        for h in range(MEM_HEADS):
            head(h, False)


def _mem_attention(proj, q_blk, gate_blk, kv, q_norm, k_norm, *, batch, tq, name):
    n = proj.shape[0]
    t = n // batch
    m = kv.shape[0] // batch
    steps = t // tq
    return pl.pallas_call(
        _mem_kernel,
        grid=(batch, steps),
        in_specs=[
            pl.BlockSpec((tq, MEM_W), lambda b, i: (b * steps + i, q_blk)),
            pl.BlockSpec((tq, MEM_W), lambda b, i: (b * steps + i, gate_blk)),
            pl.BlockSpec((m, MEM_W), lambda b, i: (b, 0)),
            pl.BlockSpec((m, MEM_W), lambda b, i: (b, 1)),
            pl.BlockSpec((1, HEAD_DIM), lambda b, i: (0, 0)),
            pl.BlockSpec((1, HEAD_DIM), lambda b, i: (0, 0)),
        ],
        out_specs=pl.BlockSpec((tq, MEM_W), lambda b, i: (b * steps + i, 0)),
        out_shape=jax.ShapeDtypeStruct((n, MEM_W), BF16),
        compiler_params=_cparams("parallel", "parallel"),
        name=name,
    )(proj, proj, kv, kv, q_norm.reshape(1, HEAD_DIM), k_norm.reshape(1, HEAD_DIM))


def _na_bias_kernel(rpb_ref, o_ref, *, n_rows):
    h = pl.program_id(0)
    n_dc = 2 * NA_WIN_W - 1
    qc = lax.broadcasted_iota(jnp.int32, (GRID_W, GRID_W), 0)
    kc = lax.broadcasted_iota(jnp.int32, (GRID_W, GRID_W), 1)
    dcm = jnp.clip(kc - qc, -(NA_WIN_W - 1), NA_WIN_W - 1) + (NA_WIN_W - 1)
    c0 = jnp.clip(qc - NA_WIN_W // 2, 0, GRID_W - NA_WIN_W)
    col_ok = (kc >= c0) & (kc < c0 + NA_WIN_W)
    neg = jnp.full((GRID_W, GRID_W), -jnp.inf, F32)
    tiles = []
    for dr in range(2 * NA_WIN_H - 1):
        t = jnp.zeros((GRID_W, GRID_W), F32)
        for dc in range(n_dc):
            t = jnp.where(dcm == dc, rpb_ref[h, dr * n_dc + dc], t)
        tiles.append(jnp.where(col_ok, t * LOG2E, neg))
    n_blocks = n_rows // NA_QROWS
    for case, j in enumerate((0, 1, n_blocks - 1)):
        start = min(max(NA_QROWS * j - NA_QROWS, 0), n_rows - NA_KROWS)
        for a in range(NA_QROWS):
            qr = NA_QROWS * j + a
            r0 = min(max(qr - NA_WIN_H // 2, 0), n_rows - NA_WIN_H)
            for b in range(NA_KROWS):
                kr = start + b
                tile = tiles[kr - qr + NA_WIN_H - 1] if r0 <= kr < r0 + NA_WIN_H else neg
                o_ref[0, case, a * GRID_W:(a + 1) * GRID_W, b * GRID_W:(b + 1) * GRID_W] = tile


def _na_bias(rpb, n_rows):
    h = rpb.shape[0]
    return pl.pallas_call(
        functools.partial(_na_bias_kernel, n_rows=n_rows),
        grid=(h,),
        in_specs=[pl.BlockSpec(memory_space=pltpu.SMEM)],
        out_specs=pl.BlockSpec((1, 3, NA_TQ, NA_TK), lambda i: (i, 0, 0, 0)),
        out_shape=jax.ShapeDtypeStruct((h, 3, NA_TQ, NA_TK), F32),
        compiler_params=_cparams("parallel"),
        name="na_bias",
    )(rpb.reshape(h, -1))


def _na_kernel(q_ref, k_ref, v_ref, gate_ref, bias_ref, qn_ref, kn_ref, o_ref, k_s, *, n_blocks):
    k_s[...] = _rms(k_ref[...].astype(F32), kn_ref[...]).astype(BF16)

    def key_start(j):
        return min(max(j - 1, 0), n_blocks - NA_KROWS // NA_QROWS) * NA_TQ

    def scores(j):
        q = _rms(q_ref[j * NA_TQ:(j + 1) * NA_TQ, :].astype(F32), qn_ref[...])
        q16 = (q * (HEAD_DIM ** -0.5 * LOG2E)).astype(BF16)
        case = 0 if j == 0 else (2 if j == n_blocks - 1 else 1)
        return _dot_nt(q16, k_s[key_start(j):key_start(j) + NA_TK, :]) + bias_ref[0, case]

    s_next = scores(0)
    for j in range(n_blocks):
        s = s_next
        if j + 1 < n_blocks:
            s_next = scores(j + 1)
        rows = slice(j * NA_TQ, (j + 1) * NA_TQ)
        e = jnp.exp2(s - jnp.max(s, axis=-1, keepdims=True))
        inv_l = 1.0 / jnp.sum(e, axis=-1, keepdims=True)
        o = _dot(e.astype(BF16), v_ref[key_start(j):key_start(j) + NA_TK, :]) * inv_l
        o_ref[rows, :] = (o * _silu(gate_ref[rows, :].astype(F32))).astype(o_ref.dtype)


def _na_attention(proj, bias, q_norm, k_norm, *, batch):
    n = proj.shape[0]
    t = n // batch
    n_blocks = t // NA_TQ
    nh = NA_HEADS
    head_block = lambda off: pl.BlockSpec((t, HEAD_DIM), lambda b, h: (b, off + h))
    return pl.pallas_call(
        functools.partial(_na_kernel, n_blocks=n_blocks),
        grid=(batch, nh),
        in_specs=[
            head_block(0),
            head_block(nh),
            head_block(2 * nh),
            head_block(3 * nh),
            pl.BlockSpec((1, 3, NA_TQ, NA_TK), lambda b, h: (h, 0, 0, 0)),
            pl.BlockSpec((1, HEAD_DIM), lambda b, h: (0, 0)),
            pl.BlockSpec((1, HEAD_DIM), lambda b, h: (0, 0)),
        ],
        out_specs=head_block(0),
        out_shape=jax.ShapeDtypeStruct((n, NA_W), BF16),
        scratch_shapes=[pltpu.VMEM((t, HEAD_DIM), BF16)],
        compiler_params=_cparams("parallel", "parallel"),
        name="na_attention",
    )(proj, proj, proj, proj, bias, q_norm.reshape(1, HEAD_DIM), k_norm.reshape(1, HEAD_DIM))


def _split3(x):
    hi = x.astype(BF16)
    r = x - hi.astype(F32)
    mid = r.astype(BF16)
    lo = (r - mid.astype(F32)).astype(BF16)
    return hi, mid, lo


def _dn_gates_kernel(ba_ref, alog_ref, dtb_ref, col_ref, row_ref, mask_s):
    tm = ba_ref.shape[0]
    ba = ba_ref[...]
    lane = lax.broadcasted_iota(jnp.int32, (1, LANES), 1)
    beta = 1.0 / (1.0 + jnp.exp(-ba))
    z = ba + dtb_ref[...]
    softplus = jnp.maximum(z, 0.0) + jnp.log(1.0 + jnp.exp(-jnp.abs(z)))
    g = -jnp.exp(alog_ref[...]) * softplus

    @pl.when((pl.program_id(0) == 0) & (pl.program_id(1) == 0))
    def _():
        r = lax.broadcasted_iota(jnp.int32, (tm, tm), 0)
        c = lax.broadcasted_iota(jnp.int32, (tm, tm), 1)
        same = (r // DN_CHUNK) == (c // DN_CHUNK)
        mask_s[0] = jnp.where(same & (c <= r), 1.0, 0.0).astype(BF16)
        mask_s[1] = jnp.where(same & (c >= r), 1.0, 0.0).astype(BF16)

    parts = _split3(g)
    m_f = mask_s[0]
    m_b = mask_s[1]
    g_f = _dot(m_f, parts[0]) + _dot(m_f, parts[1]) + _dot(m_f, parts[2])
    g_b = _dot(m_b, parts[0]) + _dot(m_b, parts[1]) + _dot(m_b, parts[2])
    nh = DN_HEADS
    out = jnp.where(lane < 2 * nh, beta, jnp.where(lane < 3 * nh, g_f, jnp.where(lane < 4 * nh, g_b, 0.0)))
    col_ref[...] = out
    for chunk in range(tm // DN_CHUNK):
        slab = out[chunk * DN_CHUNK:(chunk + 1) * DN_CHUNK, :].T
        row_ref[0, :, chunk, :] = jnp.concatenate([slab, slab], axis=1)


def _dn_gates(ba, a_log, dt_bias, *, batch, tm=512):
    n = ba.shape[0]
    t = n // batch
    steps = t // tm
    pad = jnp.zeros((2 * DN_HEADS,), F32)
    alog_row = jnp.concatenate([pad, a_log.reshape(-1).astype(F32), jnp.zeros((LANES - 4 * DN_HEADS,), F32)])
    dtb_row = jnp.concatenate([pad, dt_bias.reshape(-1).astype(F32), jnp.zeros((LANES - 4 * DN_HEADS,), F32)])
    return pl.pallas_call(
        _dn_gates_kernel,
        grid=(batch, steps),
        in_specs=[
            pl.BlockSpec((tm, LANES), lambda b, i: (b * steps + i, 0)),
            pl.BlockSpec((1, LANES), lambda b, i: (0, 0)),
            pl.BlockSpec((1, LANES), lambda b, i: (0, 0)),
        ],
        out_specs=[
            pl.BlockSpec((tm, LANES), lambda b, i: (b * steps + i, 0)),
            pl.BlockSpec((1, LANES, tm // DN_CHUNK, 2 * DN_CHUNK), lambda b, i: (b, 0, i, 0)),
        ],
        out_shape=[
            jax.ShapeDtypeStruct((n, LANES), F32),
            jax.ShapeDtypeStruct((batch, LANES, t // DN_CHUNK, 2 * DN_CHUNK), F32),
        ],
        scratch_shapes=[pltpu.VMEM((2, tm, tm), BF16)],
        compiler_params=_cparams("arbitrary", "arbitrary"),
        name="dn_gates",
    )(ba, alog_row.reshape(1, LANES), dtb_row.reshape(1, LANES))


def _dn_kernel(q_ref, k_ref, v_ref, gate_ref, cwq_ref, cwk_ref, cwv_ref, gc_ref, grf_ref, grb_ref,
               on_ref, o_ref, xq_s, xk_s, xv_s, qn_s, kn_s, vn_s, mq_s, n_s, of_s, ob_s):
    t = q_ref.shape[0]
    c_sz = DN_CHUNK
    n_chunks = t // c_sz
    head = pl.program_id(1)

    border = jnp.zeros((8, HEAD_DIM), F32)
    for x_s, x_ref in ((xq_s, q_ref), (xk_s, k_ref), (xv_s, v_ref)):
        x_s[0:8, :] = border
        x_s[t + 8:t + 16, :] = border
        x_s[8:t + 8, :] = x_ref[...].astype(F32)

    def conv_silu(x_s, cw_ref, c):
        base = pl.multiple_of(c * c_sz, c_sz)
        acc = None
        for tap in range(DN_CONV):
            term = x_s[pl.ds(base + (8 + tap - DN_CONV // 2), c_sz), :] * cw_ref[tap:tap + 1, :]
            acc = term if acc is None else acc + term
        return _silu(acc)

    def l2n(x):
        return x * lax.rsqrt(jnp.sum(x * x, axis=-1, keepdims=True) + EPS)

    def chunk_inputs(c, rows, compute):
        if not compute:
            return qn_s[rows, :], kn_s[rows, :], vn_s[rows, :]
        q = l2n(conv_silu(xq_s, cwq_ref, c)) * HEAD_DIM ** -0.5
        k = l2n(conv_silu(xk_s, cwk_ref, c))
        v = conv_silu(xv_s, cwv_ref, c)
        qn_s[rows, :] = q
        kn_s[rows, :] = k
        vn_s[rows, :] = v
        return q, k, v

    lane = lax.broadcasted_iota(jnp.int32, (1, LANES), 1)
    ii = lax.broadcasted_iota(jnp.int32, (c_sz, 2 * c_sz), 0)
    lane2 = lax.broadcasted_iota(jnp.int32, (c_sz, 2 * c_sz), 1)
    jj = lane2 & (c_sz - 1)
    right = lane2 >= c_sz
    eye_left = jnp.where(ii == lane2, 1.0, 0.0)
    zero_rows = jnp.zeros((c_sz, 2 * c_sz), BF16)

    n_levels = int(math.log2(c_sz))
    n_groups = n_chunks // DN_UNROLL

    def chunk_local(groups, compute):
        chains = []
        for u in range(DN_UNROLL):
            for d, gr_ref in enumerate((grf_ref, grb_ref)):
                c = groups[d] * DN_UNROLL + u
                rows = pl.ds(pl.multiple_of(c * c_sz, c_sz), c_sz)
                qc, kc, vc = chunk_inputs(c, rows, compute)
                gcb = gc_ref[rows, :]
                k16 = kc.astype(BF16)
                kk = _dot_nt(k16, jnp.concatenate([k16, k16], axis=0))
                qk_raw = _dot_nt(qc.astype(BF16), k16)
                beta = jnp.sum(jnp.where(lane == d * DN_HEADS + head, gcb, 0.0), axis=1, keepdims=True)
                g_col = jnp.sum(jnp.where(lane == (2 + d) * DN_HEADS + head, gcb, 0.0), axis=1, keepdims=True)
                g_row = gr_ref[0, 0, pl.ds(c, 1), :]
                incl = (ii >= jj) if d == 0 else (ii <= jj)
                strict = (ii > jj) if d == 0 else (ii < jj)
                g_edge = g_row[:, c_sz - 1:c_sz] if d == 0 else g_row[:, 0:1]
                decay = jnp.where(incl, jnp.exp(jnp.where(incl, g_col - g_row, 0.0)), 0.0)
                e_g = jnp.exp(g_col)
                q_dec = qc * e_g
                kd = kc * jnp.exp(g_edge - g_col)
                qk = qk_raw * decay[:, :c_sz]
                lhs = jnp.concatenate([kd.T.astype(BF16), qk.astype(BF16)], axis=0)
                z = jnp.where(right & strict, -(beta * kk * decay), 0.0)
                rhs = jnp.concatenate([vc * beta, kc * (beta * e_g)], axis=1)
                chains.append([d, c, rows, None, z, lhs, q_dec, rhs])
        return chains

    n_stages = n_levels + 2

    def stage_dots(chains, stage):
        if stage < n_levels:
            return [_dot(z.astype(BF16), jnp.concatenate([zero_rows, (z + eye_left).astype(BF16)], axis=0))
                    for _, _, _, _, z, _, _, _ in chains]
        if stage == n_levels:
            return [_dot(z.astype(BF16), jnp.concatenate([rhs.astype(BF16), jnp.zeros_like(rhs, BF16)], axis=0))
                    for _, _, _, _, z, _, _, rhs in chains]
        return [_dot(lhs, sol.astype(BF16)) for _, _, _, _, sol, lhs, _, _ in chains]

    def stage_update(chains, stage, outs):
        for chain, out in zip(chains, outs):
            d, c, rows, _, z, _, q_dec, rhs = chain
            if stage < n_levels:
                chain[4] = jnp.where(right, 0.0, z) + out
            elif stage == n_levels:
                chain[4] = rhs + out
            else:
                n_s[d, c] = out[:HEAD_DIM, :HEAD_DIM]
                mq_s[d, c, 0:HEAD_DIM, :] = out[:HEAD_DIM, HEAD_DIM:].astype(BF16)
                mq_s[d, c, HEAD_DIM:HEAD_DIM + c_sz, :] = (q_dec - out[HEAD_DIM:, HEAD_DIM:]).astype(BF16)
                (of_s, ob_s)[d][rows, :] = out[HEAD_DIM:, :HEAD_DIM]

    def scan_chunks(groups, u):
        return (groups[0] * DN_UNROLL + u, groups[1] * DN_UNROLL + DN_UNROLL - 1 - u)

    def scan_dots(cs, states):
        return [_dot(mq_s[d, cs[d]], states[d].astype(BF16)) for d in range(2)]

    def scan_update(cs, states, ys):
        new_states = []
        for d, (gr_ref, o_s) in enumerate(((grf_ref, of_s), (grb_ref, ob_s))):
            rows = pl.ds(pl.multiple_of(cs[d] * c_sz, c_sz), c_sz)
            g_row = gr_ref[0, 0, pl.ds(cs[d], 1), :]
            g_edge = g_row[:, c_sz - 1:c_sz] if d == 0 else g_row[:, 0:1]
            o_s[rows, :] = o_s[rows, :] + ys[d][HEAD_DIM:]
            new_states.append(states[d] * jnp.exp(g_edge) + n_s[d, cs[d]] - ys[d][:HEAD_DIM])
        return tuple(new_states)

    def fused_body(i, states, compute):
        assert n_stages == DN_UNROLL
        scan_groups = (i - 1, n_groups - i)
        chains = chunk_local((i, n_groups - 1 - i), compute)
        for stage in range(n_stages):
            cs = scan_chunks(scan_groups, stage)
            ys = scan_dots(cs, states)
            outs = stage_dots(chains, stage)
            states = scan_update(cs, states, ys)
            stage_update(chains, stage, outs)
        return states

    assert n_groups % 2 == 0
    zero = jnp.zeros((HEAD_DIM, HEAD_DIM), F32)
    first = chunk_local((0, n_groups - 1), True)
    for stage in range(n_stages):
        stage_update(first, stage, stage_dots(first, stage))
    states = lax.fori_loop(1, n_groups // 2, functools.partial(fused_body, compute=True), (zero, zero))
    states = lax.fori_loop(n_groups // 2, n_groups, functools.partial(fused_body, compute=False), states)

    def finish(start, size):
        o = of_s[start:start + size, :] + ob_s[start:start + size, :]
        o = o * lax.rsqrt(_rowsum_mxu(o * o) * (1.0 / HEAD_DIM) + EPS) * on_ref[...]
        gate = gate_ref[start:start + size, :].astype(F32)
        o_ref[start:start + size, :] = (o * _silu(gate)).astype(o_ref.dtype)

    group_rows = DN_UNROLL * c_sz
    piece = (t - 2 * group_rows) // DN_UNROLL
    for u in range(DN_UNROLL):
        cs = scan_chunks((n_groups - 1, 0), u)
        states = scan_update(cs, states, scan_dots(cs, states))
        finish(group_rows + u * piece, piece)
    finish(0, group_rows)
    finish(t - group_rows, group_rows)


def _deltanet(proj, conv_w, gates_col, gates_row, out_norm, *, batch, col0):
    n = proj.shape[0]
    t = n // batch
    n_chunks = t // DN_CHUNK
    nh = DN_HEADS
    gr = gates_row
    return pl.pallas_call(
        _dn_kernel,
        grid=(batch, nh),
        in_specs=[
            pl.BlockSpec((t, HEAD_DIM), lambda b, h: (b, col0 + h)),
            pl.BlockSpec((t, HEAD_DIM), lambda b, h: (b, col0 + nh + h)),
            pl.BlockSpec((t, HEAD_DIM), lambda b, h: (b, col0 + 2 * nh + h)),
            pl.BlockSpec((t, HEAD_DIM), lambda b, h: (b, col0 + 3 * nh + h)),
            pl.BlockSpec((DN_CONV, HEAD_DIM), lambda b, h: (0, h)),
            pl.BlockSpec((DN_CONV, HEAD_DIM), lambda b, h: (0, nh + h)),
            pl.BlockSpec((DN_CONV, HEAD_DIM), lambda b, h: (0, 2 * nh + h)),
            pl.BlockSpec((t, LANES), lambda b, h: (b, 0)),
            pl.BlockSpec((1, 1, n_chunks, 2 * DN_CHUNK), lambda b, h: (b, 2 * nh + h, 0, 0)),
            pl.BlockSpec((1, 1, n_chunks, 2 * DN_CHUNK), lambda b, h: (b, 3 * nh + h, 0, 0)),
            pl.BlockSpec((1, HEAD_DIM), lambda b, h: (0, 0)),
        ],
        out_specs=pl.BlockSpec((t, HEAD_DIM), lambda b, h: (b, h)),
        out_shape=jax.ShapeDtypeStruct((n, DN_W), BF16),
        scratch_shapes=[
            pltpu.VMEM((t + 16, HEAD_DIM), F32),
            pltpu.VMEM((t + 16, HEAD_DIM), F32),
            pltpu.VMEM((t + 16, HEAD_DIM), F32),
            pltpu.VMEM((t, HEAD_DIM), F32),
            pltpu.VMEM((t, HEAD_DIM), F32),
            pltpu.VMEM((t, HEAD_DIM), F32),
            pltpu.VMEM((2, n_chunks, HEAD_DIM + DN_CHUNK, HEAD_DIM), BF16),
            pltpu.VMEM((2, n_chunks, HEAD_DIM, HEAD_DIM), F32),
            pltpu.VMEM((t, HEAD_DIM), F32),
            pltpu.VMEM((t, HEAD_DIM), F32),
        ],
        compiler_params=_cparams("parallel", "parallel"),
        name="deltanet",
    )(proj, proj, proj, proj, conv_w, conv_w, conv_w, gates_col, gr, gr, out_norm.reshape(1, HEAD_DIM))


def _rope_tables(t):
    inv = ROPE_THETA ** (-np.arange(0, HEAD_DIM, 2, dtype=np.float64) / HEAD_DIM)
    ang = (np.arange(t, dtype=np.float32)[:, None] * inv.astype(np.float32)[None, :]).astype(np.float64)
    cos, sin = np.cos(ang), np.sin(ang)
    cos_full = np.concatenate([cos, cos], axis=1).astype(np.float32)
    sin_signed = np.concatenate([-sin, sin], axis=1).astype(np.float32)
    return jnp.asarray(cos_full), jnp.asarray(sin_signed)


def _rope(x, cos, sin_signed):
    return x * cos + pltpu.roll(x, HEAD_DIM // 2, 1) * sin_signed


def _diff_kernel(q_ref, k_ref, v_ref, gate_ref, cosq_ref, sinq_ref, cosk_ref, sink_ref, qn_ref, kn_ref,
                 lam_ref, sub_ref, o_ref, k_s, e_s, *, lam_init):
    d = HEAD_DIM
    t = k_ref.shape[0]
    tq = q_ref.shape[0]
    n_kc = t // DIFF_KC

    @pl.when(pl.program_id(2) == 0)
    def _():
        for m in range(2):
            k = _rms(k_ref[:, m * d:(m + 1) * d].astype(F32), kn_ref[...])
            k_s[m] = _rope(k, cosk_ref[...], sink_ref[...]).astype(BF16)

    lv = lam_ref[...]
    lam = (jnp.exp(jnp.sum(lv[0:1] * lv[1:2], axis=-1, keepdims=True))
           - jnp.exp(jnp.sum(lv[2:3] * lv[3:4], axis=-1, keepdims=True)) + lam_init)

    def scores(q16, m, c):
        return _dot_nt(q16, k_s[m, c * DIFF_KC:(c + 1) * DIFF_KC, :])

    q16s = []
    for m in range(2):
        q = _rms(q_ref[:, m * d:(m + 1) * d].astype(F32), qn_ref[...])
        q = _rope(q, cosq_ref[...], sinq_ref[...]) * (d ** -0.5 * LOG2E)
        q16s.append(q.astype(BF16))

    g_max = lambda ref: jnp.max(jnp.abs(ref[...]), axis=-1, keepdims=True)
    bound = g_max(qn_ref) * g_max(kn_ref) * (d * d ** -0.5 * LOG2E)

    def exact_max():
        out = []
        for m in range(2):
            mx = jnp.max(scores(q16s[m], m, 0), axis=-1, keepdims=True)
            for c in range(1, n_kc):
                mx = jnp.maximum(mx, jnp.max(scores(q16s[m], m, c), axis=-1, keepdims=True))
            out.append(mx)
        return out

    shifts = lax.cond(bound[0, 0] > DIFF_BOUND_LIMIT, exact_max,
                      lambda: [jnp.broadcast_to(bound, (tq, 1))] * 2)

    def score_pair(c):
        return [scores(q16s[m], m, c) for m in range(2)]

    lsum = [None, None]
    s_next = score_pair(0)
    for c in range(n_kc):
        s_cur = s_next
        if c + 1 < n_kc:
            s_next = score_pair(c + 1)
        for m in range(2):
            e = jnp.exp2(s_cur[m] - shifts[m])
            part = e[:, 0:LANES]
            for j in range(1, DIFF_KC // LANES):
                part = part + e[:, j * LANES:(j + 1) * LANES]
            lsum[m] = part if lsum[m] is None else lsum[m] + part
            e_s[m, :, c * DIFF_KC:(c + 1) * DIFF_KC] = e.astype(BF16)
    l0 = jnp.sum(lsum[0], axis=-1, keepdims=True)
    l1 = jnp.sum(lsum[1], axis=-1, keepdims=True)
    r16 = (lam * l0 / l1).astype(BF16)
    half = tq // 2

    def mix_pair(c):
        cols = slice(c * DIFF_KC, (c + 1) * DIFF_KC)
        return [e_s[0, r * half:(r + 1) * half, cols] - r16[r * half:(r + 1) * half] *
                e_s[1, r * half:(r + 1) * half, cols] for r in range(2)]

    accs = [None, None]
    a_next = mix_pair(0)
    for c in range(n_kc):
        a_cur = a_next
        if c + 1 < n_kc:
            a_next = mix_pair(c + 1)
        for r in range(2):
            pv = _dot(a_cur[r], v_ref[c * DIFF_KC:(c + 1) * DIFF_KC, :])
            accs[r] = pv if accs[r] is None else accs[r] + pv
    o = jnp.concatenate(accs, axis=0) * (1.0 / l0)
    o = _rms(o, sub_ref[...]) * (1.0 - lam_init)
    o_ref[...] = (o * _silu(gate_ref[...].astype(F32))).astype(o_ref.dtype)


def _diff_attention(proj, q_norm, k_norm, lam_vecs, subln, *, batch, layer_idx, tq):
    n = proj.shape[0]
    t = n // batch
    steps = t // tq
    nh = DIFF_HEADS
    w = DIFF_VDIM
    cos, sin = _rope_tables(t)
    lam_init = 0.8 - 0.6 * math.exp(-0.3 * layer_idx)
    const = lambda shape: pl.BlockSpec(shape, lambda b, h, i: (0, 0))
    return pl.pallas_call(
        functools.partial(_diff_kernel, lam_init=lam_init),
        grid=(batch, nh, steps),
        in_specs=[
            pl.BlockSpec((tq, w), lambda b, h, i: (b * steps + i, h)),
            pl.BlockSpec((t, w), lambda b, h, i: (b, nh + h)),
            pl.BlockSpec((t, w), lambda b, h, i: (b, 2 * nh + h)),
            pl.BlockSpec((tq, w), lambda b, h, i: (b * steps + i, 3 * nh + h)),
            pl.BlockSpec((tq, HEAD_DIM), lambda b, h, i: (i, 0)),
            pl.BlockSpec((tq, HEAD_DIM), lambda b, h, i: (i, 0)),
            const((t, HEAD_DIM)),
            const((t, HEAD_DIM)),
            const((1, HEAD_DIM)),
            const((1, HEAD_DIM)),
            const((4, HEAD_DIM)),
            const((1, w)),
        ],
        out_specs=pl.BlockSpec((tq, w), lambda b, h, i: (b * steps + i, h)),
        out_shape=jax.ShapeDtypeStruct((n, DIFF_W), BF16),
        scratch_shapes=[
            pltpu.VMEM((2, t, HEAD_DIM), BF16),
            pltpu.VMEM((2, tq, t), BF16),
        ],
        compiler_params=_cparams("parallel", "parallel", "arbitrary"),
        name="diff_attention",
    )(proj, proj, proj, proj, cos, sin, cos, sin, q_norm.reshape(1, HEAD_DIM), k_norm.reshape(1, HEAD_DIM),
      lam_vecs.astype(F32), subln.reshape(1, w))


def _even_layer(x2d, mem2d, mem_g, ln_g, w_in, w_mem_kv, w_out, na_q_norm, na_k_norm, na_rpb, dn_conv_w,
                dn_a_log, dn_dt_bias, dn_out_norm, mem_q_norm, mem_k_norm, *, batch):
    t = x2d.shape[0] // batch
    ba0 = 4 * NA_W + 4 * DN_W
    ba1 = ba0 + 4 * DN_HEADS
    tn = 2 * MEM_W
    w16 = w_in.astype(BF16)
    w_mem = w16[:, ba1:]
    w_ba = jnp.pad(w16[:, ba0:ba1], ((0, 0), (0, LANES - 4 * DN_HEADS)))
    proj, ba = _proj(x2d, ln_g, [(w16, 0, ba0 // tn), (w_mem, 0, 1)], w_ba, tm=1024, tn=tn, name="even_in_proj")
    kv = _proj(mem2d, mem_g, [(w_mem_kv.astype(BF16), 0, 1)], tm=mem2d.shape[0], tn=tn, name="even_mem_kv")

    bias = _na_bias(na_rpb.astype(F32), t // GRID_W)
    na_o = _na_attention(proj, bias, na_q_norm, na_k_norm, batch=batch)
    g_col, g_row = _dn_gates(ba, dn_a_log, dn_dt_bias, batch=batch)
    dn_o = _deltanet(proj, dn_conv_w.astype(F32), g_col, g_row, dn_out_norm, batch=batch,
                     col0=4 * NA_W // HEAD_DIM)
    mem_blk = ba0 // MEM_W
    mem_o = _mem_attention(proj, mem_blk, mem_blk + 1, kv, mem_q_norm, mem_k_norm, batch=batch, tq=2048,
                           name="even_mem_attention")
    return _outproj(x2d, [na_o, dn_o, mem_o], w_out.astype(BF16), tm=512, name="even_out_proj")


def _odd_layer(x2d, mem2d, mem_g, layer_idx, ln_g, w_in, w_mem_kv, w_out, q_norm, k_norm, lam_vecs, subln_g,
               mem_q_norm, mem_k_norm, *, batch):
    tn = 2 * MEM_W
    proj = _proj(x2d, ln_g, [(w_in.astype(BF16), 0, w_in.shape[1] // tn)], tm=1024, tn=tn, name="odd_in_proj")
    kv = _proj(mem2d, mem_g, [(w_mem_kv.astype(BF16), 0, 1)], tm=mem2d.shape[0], tn=tn, name="odd_mem_kv")
    diff_o = _diff_attention(proj, q_norm, k_norm, lam_vecs, subln_g, batch=batch, layer_idx=layer_idx, tq=1024)
    mem_blk = 4 * DIFF_W // MEM_W
    mem_o = _mem_attention(proj, mem_blk, mem_blk + 1, kv, mem_q_norm, mem_k_norm, batch=batch, tq=2048,
                           name="odd_mem_attention")
    return _outproj(x2d, [diff_o, mem_o], w_out.astype(BF16), tm=512, name="odd_out_proj")


def kernel(x, mem, mem_norm_g, e_ln_g, e_w_in, e_w_mem_kv, e_w_out, na_q_norm, na_k_norm, na_rpb, dn_conv_w,
           dn_a_log, dn_dt_bias, dn_out_norm, e_mem_q_norm, e_mem_k_norm, o_ln_g, o_w_in, o_w_mem_kv, o_w_out,
           df_q_norm, df_k_norm, df_lambda, df_subln, o_mem_q_norm, o_mem_k_norm):
    batch, t, d = x.shape
    depth = e_ln_g.shape[0] + o_ln_g.shape[0]
    x2d = x.reshape(batch * t, d)
    mem2d = mem.reshape(-1, d)
    for layer in range(depth):
        i = layer // 2
        if layer % 2 == 0:
            x2d = _even_layer(x2d, mem2d, mem_norm_g, e_ln_g[i], e_w_in[i], e_w_mem_kv[i], e_w_out[i],
                              na_q_norm[i], na_k_norm[i], na_rpb[i], dn_conv_w[i], dn_a_log[i], dn_dt_bias[i],
                              dn_out_norm[i], e_mem_q_norm[i], e_mem_k_norm[i], batch=batch)
        else:
            x2d = _odd_layer(x2d, mem2d, mem_norm_g, layer, o_ln_g[i], o_w_in[i], o_w_mem_kv[i], o_w_out[i],
                             df_q_norm[i], df_k_norm[i], df_lambda[i], df_subln[i], o_mem_q_norm[i],
                             o_mem_k_norm[i], batch=batch)
    return x2d.reshape(batch, t, d)
```
